```python
import math
import jax, jax.numpy as jnp
from jax import lax
import numpy as np

D_MODEL = 1024
BATCH = 8
SEQ = 2048
DEPTH = 1

D_MIX = D_MODEL
LRU_WIDTH = D_MIX // 2
LRU_BLOCKS = 8
LRU_BLOCK = LRU_WIDTH // LRU_BLOCKS
CONV_WIDTH = 4
LRU_C = 8.0
N_HEADS = 8
N_KV_HEADS = 2
GROUP = N_HEADS // N_KV_HEADS
HEAD_DIM = 64
ATTN_WIDTH = N_HEADS * HEAD_DIM
KV_WIDTH = N_KV_HEADS * HEAD_DIM
WINDOW = 128
BLOCK = 128
N_BUCKETS = 32
MAX_DISTANCE = 128
PEER_HEADS = 8
N_KEYS = 128
N_EXPERTS = N_KEYS * N_KEYS
D_QUERY = 256
D_HALF = D_QUERY // 2
TOPK = 16
PEER_CHUNK = 128
IN_COLS = 2 * LRU_WIDTH + ATTN_WIDTH + 2 * KV_WIDTH
EPS = 1e-6
NEG_INF = -1e30
SCALE = HEAD_DIM ** -0.5

kernel_name = "hymba_rglru_swa_sink_peer"


def rmsnorm(x, g):
    xf = x.astype(jnp.float32)
    y = xf * lax.rsqrt(jnp.mean(xf * xf, axis=-1, keepdims=True) + EPS)
    return (y * g.astype(jnp.float32)).astype(x.dtype)


def t5_bucket(rel):
    n = jnp.maximum(rel, 0)
    max_exact = N_BUCKETS // 2
    nf = jnp.maximum(n, 1).astype(jnp.float32)
    large = max_exact + (jnp.log(nf / max_exact) / math.log(MAX_DISTANCE / max_exact)
                         * (N_BUCKETS - max_exact)).astype(jnp.int32)
    large = jnp.minimum(large, N_BUCKETS - 1)
    return jnp.where(n < max_exact, n, large)


def band_bias_and_mask(rel_bias, S):
    nb = S // BLOCK
    i = jnp.arange(BLOCK)[:, None]
    j = jnp.arange(2 * BLOCK)[None, :]
    rel = BLOCK + i - j
    bias = rel_bias.astype(jnp.float32)[t5_bucket(rel)]
    bias = jnp.transpose(bias, (2, 0, 1)).reshape(N_KV_HEADS, GROUP, BLOCK, 2 * BLOCK)
    kpos = jnp.arange(nb)[:, None, None] * BLOCK - BLOCK + j[None]
    mask = (rel >= 0)[None] & (rel < WINDOW)[None] & (kpos >= 0)
    return bias, mask


def rglru_mixer(xb, gb, conv_w, conv_b, w_gate_a, b_gate_a, w_gate_x, b_gate_x, lru_L):
    B, S, _ = xb.shape
    xp = jnp.pad(xb, ((0, 0), (CONV_WIDTH - 1, 0), (0, 0)))
    xc = conv_b
    for tap in range(CONV_WIDTH):
        xc = xc + xp[:, tap:tap + S] * conv_w[tap]
    xblk = xc.reshape(B, S, LRU_BLOCKS, LRU_BLOCK)
    r = jax.nn.sigmoid((jnp.einsum('bsni,nij->bsnj', xblk, w_gate_a) + b_gate_a)
                       .astype(jnp.float32)).reshape(B, S, LRU_WIDTH)
    ig = jax.nn.sigmoid((jnp.einsum('bsni,nij->bsnj', xblk, w_gate_x) + b_gate_x)
                        .astype(jnp.float32)).reshape(B, S, LRU_WIDTH)
    log_a = -LRU_C * r * jax.nn.softplus(-lru_L.astype(jnp.float32))
    a = jnp.exp(log_a)
    b = jnp.sqrt(-jnp.expm1(2.0 * log_a)) * (ig * xc.astype(jnp.float32))

    def combine(c1, c2):
        a1, b1 = c1
        a2, b2 = c2
        return a1 * a2, a2 * b1 + b2

    _, h = lax.associative_scan(combine, (a, b), axis=1)
    return (h * jax.nn.gelu(gb.astype(jnp.float32))).astype(xb.dtype)


def swa_mixer(q, k, v, q_norm_g, k_norm_g, sinks, pos_bias, mask):
    B, S, _ = q.shape
    nb = S // BLOCK
    qh = rmsnorm(q.reshape(B, S, N_HEADS, HEAD_DIM), q_norm_g).astype(jnp.float32)
    kh = rmsnorm(k.reshape(B, S, N_KV_HEADS, HEAD_DIM), k_norm_g).astype(jnp.float32)
    vh = v.reshape(B, S, N_KV_HEADS, HEAD_DIM).astype(jnp.float32)
    qb = qh.reshape(B, nb, BLOCK, N_KV_HEADS, GROUP, HEAD_DIM)

    def band(t):
        tp = jnp.pad(t, ((0, 0), (BLOCK, 0), (0, 0), (0, 0)))
        tp = tp.reshape(B, nb + 1, BLOCK, N_KV_HEADS, HEAD_DIM)
        return jnp.concatenate([tp[:, :-1], tp[:, 1:]], axis=2)

    kw, vw = band(kh), band(vh)
    s = jnp.einsum('bnqhgd,bnkhd->bnhgqk', qb, kw) * SCALE + pos_bias
    s = jnp.where(mask[None, :, None, None], s, NEG_INF)
    sink = sinks.astype(jnp.float32).reshape(N_KV_HEADS, GROUP)[None, None, :, :, None]
    m = jnp.maximum(jnp.max(s, axis=-1), sink)
    p = jnp.exp(s - m[..., None])
    denom = jnp.sum(p, axis=-1) + jnp.exp(sink - m)
    o = jnp.einsum('bnhgqk,bnkhd->bnqhgd', p / denom[..., None], vw)
    return o.reshape(B, S, ATTN_WIDTH).astype(q.dtype)


def peer(xn, w_query, sub_keys, expert_u, expert_v):
    B, S, D = xn.shape
    T = B * S
    xt = xn.reshape(T, D)
    q = (xt @ w_query).reshape(T, PEER_HEADS, 2, D_HALF).astype(jnp.float32)
    s = jnp.einsum('thcd,hcnd->thcn', q, sub_keys.astype(jnp.float32))
    s_top, i_top = lax.top_k(s, TOPK)
    cand = (s_top[:, :, 0, :, None] + s_top[:, :, 1, None, :]).reshape(T, PEER_HEADS, TOPK * TOPK)
    cand_idx = (i_top[:, :, 0, :, None] * N_KEYS + i_top[:, :, 1, None, :]).reshape(T, PEER_HEADS, TOPK * TOPK)
    best, pos = lax.top_k(cand, TOPK)
    idx = jnp.take_along_axis(cand_idx, pos, axis=-1)
    g = jax.nn.softmax(best, axis=-1)
    nchunks = T // PEER_CHUNK

    def chunk(args):
        xc, ic, gc = args
        u = expert_u[ic].astype(jnp.float32)
        act = jnp.einsum('chkd,cd->chk', u, xc.astype(jnp.float32))
        w = gc * jax.nn.gelu(act)
        v = expert_v[ic].astype(jnp.float32)
        return jnp.einsum('chk,chkd->cd', w, v)

    out = lax.map(chunk, (xt.reshape(nchunks, PEER_CHUNK, D),
                          idx.reshape(nchunks, PEER_CHUNK, PEER_HEADS, TOPK),
                          g.reshape(nchunks, PEER_CHUNK, PEER_HEADS, TOPK)))
    return out.reshape(B, S, D).astype(xn.dtype)


def setup_inputs(seed: int = 0) -> dict:
    key = jax.random.key(seed)
    ks = jax.random.split(key, 24)
    f32 = jnp.float32
    nrm = lambda k, shape, scale: jax.random.normal(k, shape, f32) * scale
    gain = lambda k, shape: 1.0 + 0.05 * jax.random.normal(k, shape, f32)
    u = jax.random.uniform(ks[9], (DEPTH, LRU_WIDTH), f32, minval=0.9, maxval=0.999)
    a0 = u ** (1.0 / LRU_C)
    lru_L = jnp.log(a0) - jnp.log1p(-a0)
    return {
        "x": nrm(ks[0], (BATCH, SEQ, D_MODEL), 1.0),
        "ln_mix_g": gain(ks[1], (DEPTH, D_MODEL)),
        "w_in": nrm(ks[2], (DEPTH, D_MODEL, IN_COLS), D_MODEL ** -0.5),
        "conv_w": nrm(ks[3], (DEPTH, CONV_WIDTH, LRU_WIDTH), CONV_WIDTH ** -0.5),
        "conv_b": nrm(ks[4], (DEPTH, LRU_WIDTH), 0.02),
        "w_gate_a": nrm(ks[5], (DEPTH, LRU_BLOCKS, LRU_BLOCK, LRU_BLOCK), LRU_BLOCK ** -0.5),
        "b_gate_a": nrm(ks[6], (DEPTH, LRU_BLOCKS, LRU_BLOCK), 0.02),
        "w_gate_x": nrm(ks[7], (DEPTH, LRU_BLOCKS, LRU_BLOCK, LRU_BLOCK), LRU_BLOCK ** -0.5),
        "b_gate_x": nrm(ks[8], (DEPTH, LRU_BLOCKS, LRU_BLOCK), 0.02),
        "lru_L": lru_L,
        "q_norm_g": gain(ks[10], (DEPTH, HEAD_DIM)),
        "k_norm_g": gain(ks[11], (DEPTH, HEAD_DIM)),
        "sinks": nrm(ks[12], (DEPTH, N_HEADS), 0.5),
        "lru_out_g": gain(ks[13], (DEPTH, LRU_WIDTH)),
        "attn_out_g": gain(ks[14], (DEPTH, ATTN_WIDTH)),
        "w_out": nrm(ks[15], (DEPTH, D_MIX, D_MODEL), D_MIX ** -0.5),
        "ln_ffn_g": gain(ks[16], (DEPTH, D_MODEL)),
        "w_query": nrm(ks[17], (DEPTH, D_MODEL, PEER_HEADS * D_QUERY), D_MODEL ** -0.5),
        "sub_keys": nrm(ks[18], (DEPTH, PEER_HEADS, 2, N_KEYS, D_HALF), D_HALF ** -0.5),
        "expert_u": nrm(ks[19], (DEPTH, N_EXPERTS, D_MODEL), D_MODEL ** -0.5),
        "expert_v": nrm(ks[20], (DEPTH, N_EXPERTS, D_MODEL), (PEER_HEADS * TOPK) ** -0.5),
        "rel_bias": nrm(ks[21], (N_BUCKETS, N_HEADS), 0.5),
    }


def reference(x, ln_mix_g, w_in, conv_w, conv_b, w_gate_a, b_gate_a, w_gate_x, b_gate_x,
              lru_L, q_norm_g, k_norm_g, sinks, lru_out_g, attn_out_g, w_out, ln_ffn_g,
              w_query, sub_keys, expert_u, expert_v, rel_bias):
    S = x.shape[1]
    pos_bias, mask = band_bias_and_mask(rel_bias, S)
    splits = [LRU_WIDTH, 2 * LRU_WIDTH, 2 * LRU_WIDTH + ATTN_WIDTH,
              2 * LRU_WIDTH + ATTN_WIDTH + KV_WIDTH]
    for l in range(DEPTH):
        h = rmsnorm(x, ln_mix_g[l])
        proj = h @ w_in[l]
        xb, gb, q, k, v = jnp.split(proj, splits, axis=-1)
        y_lru = rglru_mixer(xb, gb, conv_w[l], conv_b[l], w_gate_a[l], b_gate_a[l],
                            w_gate_x[l], b_gate_x[l], lru_L[l])
        y_att = swa_mixer(q, k, v, q_norm_g[l], k_norm_g[l], sinks[l], pos_bias, mask)
        mix = jnp.concatenate([rmsnorm(y_lru, lru_out_g[l]), rmsnorm(y_att, attn_out_g[l])], axis=-1)
        x = x + mix @ w_out[l]
        x = x + peer(rmsnorm(x, ln_ffn_g[l]), w_query[l], sub_keys[l], expert_u[l], expert_v[l])
    return x
```

```python
import functools
import math

import jax
import jax.numpy as jnp
import numpy as np
from jax import lax
from jax.experimental import pallas as pl
from jax.experimental.pallas import tpu as pltpu

D_MODEL = 1024
LRU_WIDTH = 512
LRU_BLOCKS = 8
LRU_BLOCK = LRU_WIDTH // LRU_BLOCKS
CONV_WIDTH = 4
LRU_C = 8.0
N_HEADS = 8
N_KV_HEADS = 2
GROUP = N_HEADS // N_KV_HEADS
HEAD_DIM = 64
ATTN_WIDTH = N_HEADS * HEAD_DIM
KV_WIDTH = N_KV_HEADS * HEAD_DIM
WINDOW = 128
BLOCK = 128
N_BUCKETS = 32
MAX_DISTANCE = 128
PEER_HEADS = 8
N_KEYS = 128
N_EXPERTS = N_KEYS * N_KEYS
D_QUERY = 256
D_HALF = D_QUERY // 2
TOPK = 16
N_PAIR = PEER_HEADS * TOPK
IN_COLS = 2 * LRU_WIDTH + ATTN_WIDTH + 2 * KV_WIDTH
EPS = 1e-6
NEG_INF = -1e30
SCALE = HEAD_DIM ** -0.5

F32 = jnp.float32
BF16 = jnp.bfloat16
LANES = 128
SUBLANES = 8
N_CHUNK = D_MODEL // LANES
WORD_ROWS = N_CHUNK // 2
VMEM_LIMIT = 52 * 1024 * 1024

ROW_TILE = 512
LRU_TILE = 256
TOPK_TILE = 256
PEER_TILE = 256


def _rms(x, g):
    return x * lax.rsqrt(jnp.mean(x * x, axis=-1, keepdims=True) + EPS) * g


def _gelu(x):
    return 0.5 * x * (1.0 + jnp.tanh(math.sqrt(2.0 / math.pi) * (x + 0.044715 * (x * x * x))))


def _params(*sem):
    return pltpu.CompilerParams(dimension_semantics=sem, vmem_limit_bytes=VMEM_LIMIT)


def _resident(shape):
    zeros = (0,) * len(shape)
    return pl.BlockSpec(shape, lambda *_: zeros, pipeline_mode=pl.Buffered(1))


def _inproj_kernel(x_ref, g_ref, w_ref, xb_ref, gb_ref, q_ref, k_ref, v_ref):
    h = _rms(x_ref[...], g_ref[...])
    p = jnp.dot(h.astype(BF16), w_ref[...], preferred_element_type=F32)
    c0, c1, c2, c3 = LRU_WIDTH, 2 * LRU_WIDTH, 2 * LRU_WIDTH + ATTN_WIDTH, IN_COLS - KV_WIDTH
    xb_ref[...] = p[:, :c0]
    gb_ref[...] = p[:, c0:c1]
    q_ref[...] = p[:, c1:c2]
    k_ref[...] = p[:, c2:c3]
    v_ref[...] = p[:, c3:]


def _inproj(x2, g, w_in):
    t = x2.shape[0]
    widths = (LRU_WIDTH, LRU_WIDTH, ATTN_WIDTH, KV_WIDTH, KV_WIDTH)
    row = lambda w: pl.BlockSpec((ROW_TILE, w), lambda i: (i, 0))
    return pl.pallas_call(
        _inproj_kernel,
        out_shape=[jax.ShapeDtypeStruct((t, w), F32) for w in widths],
        grid=(t // ROW_TILE,),
        in_specs=[row(D_MODEL), _resident((1, D_MODEL)), _resident((D_MODEL, IN_COLS))],
        out_specs=[row(w) for w in widths],
        compiler_params=_params("parallel"),
        name="inproj",
    )(x2, g.reshape(1, D_MODEL), w_in.astype(BF16))


def _rglru_kernel(xb_ref, gb_ref, cw_ref, cb_ref, wg_ref, bg_ref, l_ref, og_ref, o_ref,
                  xs_scr, a_scr, b_scr, h_scr):
    tt, c = xb_ref.shape

    @pl.when(pl.program_id(1) == 0)
    def _():
        xs_scr[0:SUBLANES, :] = jnp.zeros((SUBLANES, c), F32)
        h_scr[...] = jnp.zeros_like(h_scr)

    xb = xb_ref[...]
    xs_scr[SUBLANES:SUBLANES + tt, :] = xb
    xc = cb_ref[...] + xb * cw_ref[CONV_WIDTH - 1:CONV_WIDTH, :]
    for back in range(1, CONV_WIDTH):
        tap = CONV_WIDTH - 1 - back
        xc = xc + xs_scr[SUBLANES - back:SUBLANES - back + tt, :] * cw_ref[tap:tap + 1, :]
    xs_scr[0:SUBLANES, :] = xb[tt - SUBLANES:, :]

    gates = jnp.dot(xc.astype(BF16), wg_ref[...], preferred_element_type=F32) + bg_ref[...]
    r = jax.nn.sigmoid(gates[:, :c])
    ig = jax.nn.sigmoid(gates[:, c:])
    lam = l_ref[...]
    softplus_neg = jnp.maximum(-lam, 0.0) + jnp.log1p(jnp.exp(-jnp.abs(lam)))
    log_a = (-LRU_C) * r * softplus_neg
    a = jnp.exp(log_a)
    b = jnp.sqrt(-jnp.tanh(log_a) * (a * a + 1.0)) * (ig * xc)
    a_scr[...] = a
    b_scr[...] = b

    rows = lax.broadcasted_iota(jnp.int32, (SUBLANES, c), 0)

    def tile(i, h):
        off = pl.multiple_of(i * SUBLANES, SUBLANES)
        at = a_scr[pl.ds(off, SUBLANES), :]
        bt = b_scr[pl.ds(off, SUBLANES), :]
        for d in (1, 2, 4):
            keep = rows >= d
            bt = jnp.where(keep, at * pltpu.roll(bt, d, axis=0) + bt, bt)
            at = jnp.where(keep, at * pltpu.roll(at, d, axis=0), at)
        ht = at * h + bt
        b_scr[pl.ds(off, SUBLANES), :] = ht
        return ht[SUBLANES - 1:SUBLANES, :]

    h_scr[...] = lax.fori_loop(0, tt // SUBLANES, tile, h_scr[...])
    y = b_scr[...] * _gelu(gb_ref[...])
    o_ref[...] = _rms(y, og_ref[...]).astype(o_ref.dtype)


def _rglru(xb, gb, conv_w, conv_b, w_gate_a, b_gate_a, w_gate_x, b_gate_x, lru_l, out_g, batch, seq):
    c = LRU_WIDTH
    eye = jnp.eye(LRU_BLOCKS, dtype=F32)
    dense = lambda w: jnp.einsum("nij,nm->nimj", w, eye).reshape(c, c)
    wg = jnp.concatenate([dense(w_gate_a), dense(w_gate_x)], axis=1).astype(BF16)
    bg = jnp.concatenate([b_gate_a.reshape(1, c), b_gate_x.reshape(1, c)], axis=1)
    blk = pl.BlockSpec((None, LRU_TILE, c), lambda b, j: (b, j, 0))
    out = pl.pallas_call(
        _rglru_kernel,
        out_shape=jax.ShapeDtypeStruct((batch, seq, c), BF16),
        grid=(batch, seq // LRU_TILE),
        in_specs=[blk, blk, _resident((CONV_WIDTH, c)), _resident((1, c)), _resident((c, 2 * c)),
                  _resident((1, 2 * c)), _resident((1, c)), _resident((1, c))],
        out_specs=blk,
        scratch_shapes=[pltpu.VMEM((LRU_TILE + SUBLANES, c), F32), pltpu.VMEM((LRU_TILE, c), F32),
                        pltpu.VMEM((LRU_TILE, c), F32), pltpu.VMEM((1, c), F32)],
        compiler_params=_params("parallel", "arbitrary"),
        name="rglru",
    )(xb.reshape(batch, seq, c), gb.reshape(batch, seq, c), conv_w, conv_b.reshape(1, c), wg, bg,
      lru_l.reshape(1, c), out_g.reshape(1, c))
    return out.reshape(batch * seq, c)


def _bias_kernel(rb_ref, onehot_ref, o_ref):
    o_ref[...] = jnp.dot(rb_ref[...], onehot_ref[...], preferred_element_type=F32,
                         precision=lax.Precision.HIGHEST)


def _t5_bucket(rel):
    n = jnp.maximum(rel, 0)
    max_exact = N_BUCKETS // 2
    nf = jnp.maximum(n, 1).astype(F32)
    large = max_exact + (jnp.log(nf / max_exact) / math.log(MAX_DISTANCE / max_exact)
                         * (N_BUCKETS - max_exact)).astype(jnp.int32)
    large = jnp.minimum(large, N_BUCKETS - 1)
    return jnp.where(n < max_exact, n, large)


def _band_bias(rel_bias):
    i = jnp.arange(BLOCK)[:, None]
    j = jnp.arange(2 * BLOCK)[None, :]
    bucket = _t5_bucket(BLOCK + i - j).reshape(1, -1)
    onehot = (bucket == jnp.arange(N_BUCKETS)[:, None]).astype(F32)
    out = pl.pallas_call(
        _bias_kernel,
        out_shape=jax.ShapeDtypeStruct((N_HEADS, BLOCK * 2 * BLOCK), F32),
        name="band_bias",
    )(rel_bias.astype(F32).T, onehot)
    return out.reshape(N_HEADS, BLOCK, 2 * BLOCK)


def _swa_kernel(sink_ref, q_ref, kc_ref, kp_ref, vc_ref, vp_ref, bias_ref, qg_ref, kg_ref, og_ref, o_ref):
    first_key = jnp.where(pl.program_id(1) > 0, 0, BLOCK)
    kk = jnp.concatenate([kp_ref[...], kc_ref[...]], axis=0)
    vv = jnp.concatenate([vp_ref[...], vc_ref[...]], axis=0).astype(BF16)
    shape = (GROUP * BLOCK, 2 * BLOCK)
    qi = lax.broadcasted_iota(jnp.int32, shape, 0) % BLOCK
    kj = lax.broadcasted_iota(jnp.int32, shape, 1)
    valid = (kj > jnp.maximum(qi + (BLOCK - WINDOW), first_key - 1)) & (kj <= qi + BLOCK)
    outs = []
    for hk in range(N_KV_HEADS):
        cols = slice(hk * HEAD_DIM, (hk + 1) * HEAD_DIM)
        k_n = _rms(kk[:, cols], kg_ref[...]).astype(BF16)
        qs, sinks = [], []
        for g in range(GROUP):
            h = hk * GROUP + g
            qs.append(_rms(q_ref[:, h * HEAD_DIM:(h + 1) * HEAD_DIM], qg_ref[...]).astype(BF16))
            sinks.append(jnp.full((BLOCK, 1), sink_ref[h], F32))
        q_n = jnp.concatenate(qs, axis=0)
        sink = jnp.concatenate(sinks, axis=0)
        s = lax.dot_general(q_n, k_n, (((1,), (1,)), ((), ())), preferred_element_type=F32)
        s = s * SCALE + bias_ref[hk * GROUP:(hk + 1) * GROUP].reshape(shape)
        s = jnp.where(valid, s, NEG_INF)
        m = jnp.maximum(jnp.max(s, axis=-1, keepdims=True), sink)
        p = jnp.exp(s - m)
        denom = jnp.sum(p, axis=-1, keepdims=True) + jnp.exp(sink - m)
        o = jnp.dot(p.astype(BF16), vv[:, cols], preferred_element_type=F32) / denom
        outs.extend(o[g * BLOCK:(g + 1) * BLOCK] for g in range(GROUP))
    y = jnp.concatenate(outs, axis=-1)
    o_ref[...] = _rms(y, og_ref[...]).astype(o_ref.dtype)


def _swa(q, k, v, bias, q_norm_g, k_norm_g, sinks, out_g, batch, seq):
    nb = seq // BLOCK
    cur = lambda w: pl.BlockSpec((None, BLOCK, w), lambda b, n: (b, n, 0))
    prev = lambda w: pl.BlockSpec((None, BLOCK, w), lambda b, n: (b, jnp.maximum(n - 1, 0), 0))
    k3 = k.reshape(batch, seq, KV_WIDTH)
    v3 = v.reshape(batch, seq, KV_WIDTH)
    out = pl.pallas_call(
        _swa_kernel,
        out_shape=jax.ShapeDtypeStruct((batch, seq, ATTN_WIDTH), BF16),
        grid=(batch, nb),
        in_specs=[pl.BlockSpec(memory_space=pltpu.SMEM), cur(ATTN_WIDTH), cur(KV_WIDTH), prev(KV_WIDTH),
                  cur(KV_WIDTH), prev(KV_WIDTH), _resident((N_HEADS, BLOCK, 2 * BLOCK)),
                  _resident((1, HEAD_DIM)), _resident((1, HEAD_DIM)), _resident((1, ATTN_WIDTH))],
        out_specs=cur(ATTN_WIDTH),
        compiler_params=_params("parallel", "parallel"),
        name="swa",
    )(sinks.astype(F32), q.reshape(batch, seq, ATTN_WIDTH), k3, k3, v3, v3, bias,
      q_norm_g.reshape(1, HEAD_DIM), k_norm_g.reshape(1, HEAD_DIM), out_g.reshape(1, ATTN_WIDTH))
    return out.reshape(batch * seq, ATTN_WIDTH)


def _outproj_kernel(x_ref, ml_ref, ma_ref, w_ref, g_ref, x1_ref, xn_ref):
    acc = jnp.dot(ml_ref[...], w_ref[:LRU_WIDTH, :], preferred_element_type=F32)
    acc = acc + jnp.dot(ma_ref[...], w_ref[LRU_WIDTH:, :], preferred_element_type=F32)
    x1 = x_ref[...] + acc
    x1_ref[...] = x1
    xn_ref[...] = _rms(x1, g_ref[...])


def _outproj(x2, y_lru, y_att, w_out, g):
    t = x2.shape[0]
    row = lambda w: pl.BlockSpec((ROW_TILE, w), lambda i: (i, 0))
    return pl.pallas_call(
        _outproj_kernel,
        out_shape=[jax.ShapeDtypeStruct((t, D_MODEL), F32)] * 2,
        grid=(t // ROW_TILE,),
        in_specs=[row(D_MODEL), row(LRU_WIDTH), row(ATTN_WIDTH), _resident((D_MODEL, D_MODEL)),
                  _resident((1, D_MODEL))],
        out_specs=[row(D_MODEL)] * 2,
        compiler_params=_params("parallel"),
        name="outproj",
    )(x2, y_lru, y_att, w_out.astype(BF16), g.reshape(1, D_MODEL))


def _top_rows(s, k, payload=None):
    n = s.shape[0]
    rows = lax.broadcasted_iota(jnp.int32, s.shape, 0)
    vals, picks = [], []
    for _ in range(k):
        m = jnp.max(s, axis=0, keepdims=True)
        first = jnp.min(jnp.where(s == m, rows, n), axis=0, keepdims=True)
        hit = rows == first
        vals.append(m)
        if payload is None:
            picks.append(first)
        else:
            picks.append(jnp.max(jnp.where(hit, payload, -1), axis=0, keepdims=True))
        s = jnp.where(hit, -jnp.inf, s)
    return jnp.concatenate(vals, axis=0), jnp.concatenate(picks, axis=0)


def _peer_topk_kernel(xn_ref, wq_ref, keys_ref, idx_ref, g_ref, q_scr):
    q_t = lax.dot_general(wq_ref[...], xn_ref[...].astype(BF16), (((1,), (1,)), ((), ())),
                          preferred_element_type=F32)
    q_scr[...] = q_t.astype(BF16)

    def head(h, carry):
        tops = []
        for half in range(2):
            hc = h * 2 + half
            q_hc = q_scr[pl.ds(pl.multiple_of(hc * D_HALF, D_HALF), D_HALF), :]
            s = jnp.dot(keys_ref[hc], q_hc, preferred_element_type=F32)
            tops.append(_top_rows(s, TOPK))
        (s0, i0), (s1, i1) = tops
        cand = jnp.concatenate([s0[a:a + 1] + s1 for a in range(TOPK)], axis=0)
        cand_idx = jnp.concatenate([i0[a:a + 1] * N_KEYS + i1 for a in range(TOPK)], axis=0)
        best, idx = _top_rows(cand, TOPK, payload=cand_idx)
        e = jnp.exp(best - best[0:1])
        idx_ref[h] = idx
        g_ref[h] = e / jnp.sum(e, axis=0, keepdims=True)
        return carry

    lax.fori_loop(0, PEER_HEADS, head, 0)


def _peer_topk(xn, w_query, sub_keys):
    t = xn.shape[0]
    nq = PEER_HEADS * D_QUERY
    out_blk = pl.BlockSpec((PEER_HEADS, TOPK, TOPK_TILE), lambda i: (0, 0, i))
    idx_t, g_t = pl.pallas_call(
        _peer_topk_kernel,
        out_shape=[jax.ShapeDtypeStruct((PEER_HEADS, TOPK, t), jnp.int32),
                   jax.ShapeDtypeStruct((PEER_HEADS, TOPK, t), F32)],
        grid=(t // TOPK_TILE,),
        in_specs=[pl.BlockSpec((TOPK_TILE, D_MODEL), lambda i: (i, 0)), _resident((nq, D_MODEL)),
                  _resident((2 * PEER_HEADS, N_KEYS, D_HALF))],
        out_specs=[out_blk, out_blk],
        scratch_shapes=[pltpu.VMEM((nq, TOPK_TILE), BF16)],
        compiler_params=_params("parallel"),
        name="peer_topk",
    )(xn, w_query.T.astype(BF16), sub_keys.reshape(2 * PEER_HEADS, N_KEYS, D_HALF).astype(BF16))
    to_rows = lambda a: a.transpose(2, 0, 1).reshape(t, N_PAIR)
    return to_rows(idx_t), to_rows(g_t)


def _pack_table(tab):
    t = tab.astype(BF16).reshape(tab.shape[0], WORD_ROWS, 2, LANES)
    return lax.bitcast_convert_type(jnp.swapaxes(t, -1, -2), jnp.int32)


def _chunk_sum_matrix():
    return (np.arange(D_MODEL)[:, None] // N_CHUNK == np.arange(N_PAIR)[None, :]).astype(np.float32)


def _gather_rows(idx_ref, tab_ref, t, buf):
    for k in range(N_PAIR):
        buf[pl.ds(WORD_ROWS * k, WORD_ROWS), :] = tab_ref[idx_ref[t, k]]


def _gathered_rows(buf):
    return pltpu.bitcast(buf[...], BF16)


def _split_bf16(a):
    hi = a.astype(BF16)
    return jnp.concatenate([hi, (a - hi.astype(F32)).astype(BF16)], axis=0)


def _token_pipeline(n_tokens, gather, compute, buf_a, buf_b):
    gather(0, buf_a)

    def two(i, carry):
        t0 = 2 * i
        gather(t0 + 1, buf_b)
        compute(t0, buf_a)
        gather(jnp.minimum(t0 + 2, n_tokens - 1), buf_a)
        compute(t0 + 1, buf_b)
        return carry

    lax.fori_loop(0, n_tokens // 2, two, 0)


def _peer_u_kernel(idx_ref, x_ref, tab_ref, sum_ref, gate_ref, w_ref, buf_a, buf_b, z_scr):
    n_tokens = x_ref.shape[0]
    shape = (2 * N_CHUNK, D_MODEL)
    own = (lax.broadcasted_iota(jnp.int32, shape, 0) % N_CHUNK) == (lax.broadcasted_iota(jnp.int32, shape, 1) % N_CHUNK)

    def compute(t, buf):
        o = lax.dot_general(_split_bf16(x_ref[t]), _gathered_rows(buf), (((1,), (1,)), ((), ())),
                            preferred_element_type=F32)
        z_scr[pl.ds(t, 1), :] = jnp.sum(jnp.where(own, o, 0.0), axis=0, keepdims=True)

    _token_pipeline(n_tokens, functools.partial(_gather_rows, idx_ref, tab_ref), compute, buf_a, buf_b)
    act = jnp.dot(z_scr[...], sum_ref[...], preferred_element_type=F32, precision=lax.Precision.HIGHEST)
    w_ref[...] = gate_ref[...] * _gelu(act)


def _peer_v_kernel(idx_ref, w_ref, x1_ref, tab_ref, spread_ref, y_ref, buf_a, buf_b, wx_scr):
    n_tokens = x1_ref.shape[0]
    shape = (N_CHUNK, D_MODEL)
    own = lax.broadcasted_iota(jnp.int32, shape, 0) == (lax.broadcasted_iota(jnp.int32, shape, 1) % N_CHUNK)
    wx_scr[...] = jnp.dot(w_ref[...], spread_ref[...], preferred_element_type=F32,
                          precision=lax.Precision.HIGHEST)

    def compute(t, buf):
        lhs = jnp.where(own, jnp.broadcast_to(wx_scr[pl.ds(t, 1), :], shape), 0.0)
        r = jnp.dot(_split_bf16(lhs), _gathered_rows(buf), preferred_element_type=F32)
        y_ref[t] = x1_ref[t] + r[:N_CHUNK] + r[N_CHUNK:]

    _token_pipeline(n_tokens, functools.partial(_gather_rows, idx_ref, tab_ref), compute, buf_a, buf_b)


def _peer_specs(t):
    idx = pl.BlockSpec((PEER_TILE, N_PAIR), lambda i: (i, 0), memory_space=pltpu.SMEM)
    pairs = pl.BlockSpec((PEER_TILE, N_PAIR), lambda i: (i, 0))
    rows = pl.BlockSpec((PEER_TILE, N_CHUNK, LANES), lambda i: (i, 0, 0))
    table = _resident((N_EXPERTS, WORD_ROWS, LANES))
    bufs = [pltpu.VMEM((N_PAIR * WORD_ROWS, LANES), jnp.int32)] * 2 + [pltpu.VMEM((PEER_TILE, D_MODEL), F32)]
    return idx, pairs, rows, table, bufs


def _peer_u(idx, gate, xn, tab_u):
    t = idx.shape[0]
    idx_s, pairs, rows, table, bufs = _peer_specs(t)
    return pl.pallas_call(
        _peer_u_kernel,
        out_shape=jax.ShapeDtypeStruct((t, N_PAIR), F32),
        grid=(t // PEER_TILE,),
        in_specs=[idx_s, rows, table, _resident((D_MODEL, N_PAIR)), pairs],
        out_specs=pairs,
        scratch_shapes=bufs,
        compiler_params=_params("parallel"),
        name="peer_u",
    )(idx, xn.reshape(t, N_CHUNK, LANES), tab_u, jnp.asarray(_chunk_sum_matrix()), gate)


def _peer_v(idx, w, x1, tab_v):
    t = idx.shape[0]
    idx_s, pairs, rows, table, bufs = _peer_specs(t)
    y = pl.pallas_call(
        _peer_v_kernel,
        out_shape=jax.ShapeDtypeStruct((t, N_CHUNK, LANES), F32),
        grid=(t // PEER_TILE,),
        in_specs=[idx_s, pairs, rows, table, _resident((N_PAIR, D_MODEL))],
        out_specs=rows,
        scratch_shapes=bufs,
        compiler_params=_params("parallel"),
        name="peer_v",
    )(idx, w, x1.reshape(t, N_CHUNK, LANES), tab_v, jnp.asarray(_chunk_sum_matrix().T))
    return y.reshape(t, D_MODEL)


def kernel(x, ln_mix_g, w_in, conv_w, conv_b, w_gate_a, b_gate_a, w_gate_x, b_gate_x, lru_L, q_norm_g,
           k_norm_g, sinks, lru_out_g, attn_out_g, w_out, ln_ffn_g, w_query, sub_keys, expert_u, expert_v,
           rel_bias):
    batch, seq, _ = x.shape
    bias = _band_bias(rel_bias)
    x2 = x.reshape(batch * seq, D_MODEL)
    for l in range(w_in.shape[0]):
        xb, gb, q, k, v = _inproj(x2, ln_mix_g[l], w_in[l])
        y_lru = _rglru(xb, gb, conv_w[l], conv_b[l], w_gate_a[l], b_gate_a[l], w_gate_x[l], b_gate_x[l],
                       lru_L[l], lru_out_g[l], batch, seq)
        y_att = _swa(q, k, v, bias, q_norm_g[l], k_norm_g[l], sinks[l], attn_out_g[l], batch, seq)
        x1, xn = _outproj(x2, y_lru, y_att, w_out[l], ln_ffn_g[l])
        idx, gate = _peer_topk(xn, w_query[l], sub_keys[l])
        w = _peer_u(idx, gate, xn, _pack_table(expert_u[l]))
        x2 = _peer_v(idx, w, x1, _pack_table(expert_v[l]))
    return x2.reshape(batch, seq, D_MODEL)
```

```python
import functools
import math

import jax
import jax.numpy as jnp
import numpy as np
from jax import lax
from jax.experimental import pallas as pl
from jax.experimental.pallas import tpu as pltpu

D_MODEL = 1024
LRU_WIDTH = 512
LRU_BLOCKS = 8
LRU_BLOCK = LRU_WIDTH // LRU_BLOCKS
CONV_WIDTH = 4
LRU_C = 8.0
N_HEADS = 8
N_KV_HEADS = 2
GROUP = N_HEADS // N_KV_HEADS
HEAD_DIM = 64
ATTN_WIDTH = N_HEADS * HEAD_DIM
KV_WIDTH = N_KV_HEADS * HEAD_DIM
WINDOW = 128
BLOCK = 128
N_BUCKETS = 32
MAX_DISTANCE = 128
PEER_HEADS = 8
N_KEYS = 128
N_EXPERTS = N_KEYS * N_KEYS
D_QUERY = 256
D_HALF = D_QUERY // 2
TOPK = 16
N_PAIR = PEER_HEADS * TOPK
IN_COLS = 2 * LRU_WIDTH + ATTN_WIDTH + 2 * KV_WIDTH
EPS = 1e-6
NEG_INF = -1e30
SCALE = HEAD_DIM ** -0.5

F32 = jnp.float32
BF16 = jnp.bfloat16
LANES = 128
SUBLANES = 8
N_CHUNK = D_MODEL // LANES
WORD_ROWS = N_CHUNK // 2
VMEM_LIMIT = 52 * 1024 * 1024

ROW_TILE = 512
LRU_TILE = 256
TOPK_TILE = 256
PEER_HALF = 32
PEER_STEP = 2 * PEER_HALF
PEER_GROUP = 256
STEPS_PER_GROUP = PEER_GROUP // PEER_STEP


def _rms(x, g):
    return x * lax.rsqrt(jnp.mean(x * x, axis=-1, keepdims=True) + EPS) * g


def _gelu(x):
    return 0.5 * x * (1.0 + jnp.tanh(math.sqrt(2.0 / math.pi) * (x + 0.044715 * (x * x * x))))


def _params(*sem):
    return pltpu.CompilerParams(dimension_semantics=sem, vmem_limit_bytes=VMEM_LIMIT)


def _resident(shape):
    zeros = (0,) * len(shape)
    return pl.BlockSpec(shape, lambda *_: zeros, pipeline_mode=pl.Buffered(1))


def _inproj_kernel(x_ref, g_ref, w_ref, xb_ref, gb_ref, q_ref, k_ref, v_ref):
    h = _rms(x_ref[...], g_ref[...])
    p = jnp.dot(h.astype(BF16), w_ref[...], preferred_element_type=F32)
    c0, c1, c2, c3 = LRU_WIDTH, 2 * LRU_WIDTH, 2 * LRU_WIDTH + ATTN_WIDTH, IN_COLS - KV_WIDTH
    xb_ref[...] = p[:, :c0]
    gb_ref[...] = p[:, c0:c1]
    q_ref[...] = p[:, c1:c2]
    k_ref[...] = p[:, c2:c3]
    v_ref[...] = p[:, c3:]


def _inproj(x2, g, w_in):
    t = x2.shape[0]
    widths = (LRU_WIDTH, LRU_WIDTH, ATTN_WIDTH, KV_WIDTH, KV_WIDTH)
    row = lambda w: pl.BlockSpec((ROW_TILE, w), lambda i: (i, 0))
    return pl.pallas_call(
        _inproj_kernel,
        out_shape=[jax.ShapeDtypeStruct((t, w), F32) for w in widths],
        grid=(t // ROW_TILE,),
        in_specs=[row(D_MODEL), _resident((1, D_MODEL)), _resident((D_MODEL, IN_COLS))],
        out_specs=[row(w) for w in widths],
        compiler_params=_params("parallel"),
        name="inproj",
    )(x2, g.reshape(1, D_MODEL), w_in.astype(BF16))


def _rglru_kernel(xb_ref, gb_ref, cw_ref, cb_ref, wg_ref, bg_ref, l_ref, og_ref, o_ref,
                  xs_scr, a_scr, b_scr, h_scr):
    tt, c = xb_ref.shape

    @pl.when(pl.program_id(1) == 0)
    def _():
        xs_scr[0:SUBLANES, :] = jnp.zeros((SUBLANES, c), F32)
        h_scr[...] = jnp.zeros_like(h_scr)

    xb = xb_ref[...]
    xs_scr[SUBLANES:SUBLANES + tt, :] = xb
    xc = cb_ref[...] + xb * cw_ref[CONV_WIDTH - 1:CONV_WIDTH, :]
    for back in range(1, CONV_WIDTH):
        tap = CONV_WIDTH - 1 - back
        xc = xc + xs_scr[SUBLANES - back:SUBLANES - back + tt, :] * cw_ref[tap:tap + 1, :]
    xs_scr[0:SUBLANES, :] = xb[tt - SUBLANES:, :]

    gates = jnp.dot(xc.astype(BF16), wg_ref[...], preferred_element_type=F32) + bg_ref[...]
    r = jax.nn.sigmoid(gates[:, :c])
    ig = jax.nn.sigmoid(gates[:, c:])
    lam = l_ref[...]
    softplus_neg = jnp.maximum(-lam, 0.0) + jnp.log1p(jnp.exp(-jnp.abs(lam)))
    log_a = (-LRU_C) * r * softplus_neg
    a = jnp.exp(log_a)
    b = jnp.sqrt(-jnp.tanh(log_a) * (a * a + 1.0)) * (ig * xc)
    a_scr[...] = a
    b_scr[...] = b

    rows = lax.broadcasted_iota(jnp.int32, (SUBLANES, c), 0)

    def tile(i, h):
        off = pl.multiple_of(i * SUBLANES, SUBLANES)
        at = a_scr[pl.ds(off, SUBLANES), :]
        bt = b_scr[pl.ds(off, SUBLANES), :]
        for d in (1, 2, 4):
            keep = rows >= d
            bt = jnp.where(keep, at * pltpu.roll(bt, d, axis=0) + bt, bt)
            at = jnp.where(keep, at * pltpu.roll(at, d, axis=0), at)
        ht = at * h + bt
        b_scr[pl.ds(off, SUBLANES), :] = ht
        return ht[SUBLANES - 1:SUBLANES, :]

    h_scr[...] = lax.fori_loop(0, tt // SUBLANES, tile, h_scr[...])
    y = b_scr[...] * _gelu(gb_ref[...])
    o_ref[...] = _rms(y, og_ref[...]).astype(o_ref.dtype)


def _rglru(xb, gb, conv_w, conv_b, w_gate_a, b_gate_a, w_gate_x, b_gate_x, lru_l, out_g, batch, seq):
    c = LRU_WIDTH
    eye = jnp.eye(LRU_BLOCKS, dtype=F32)
    dense = lambda w: jnp.einsum("nij,nm->nimj", w, eye).reshape(c, c)
    wg = jnp.concatenate([dense(w_gate_a), dense(w_gate_x)], axis=1).astype(BF16)
    bg = jnp.concatenate([b_gate_a.reshape(1, c), b_gate_x.reshape(1, c)], axis=1)
    blk = pl.BlockSpec((None, LRU_TILE, c), lambda b, j: (b, j, 0))
    out = pl.pallas_call(
        _rglru_kernel,
        out_shape=jax.ShapeDtypeStruct((batch, seq, c), BF16),
        grid=(batch, seq // LRU_TILE),
        in_specs=[blk, blk, _resident((CONV_WIDTH, c)), _resident((1, c)), _resident((c, 2 * c)),
                  _resident((1, 2 * c)), _resident((1, c)), _resident((1, c))],
        out_specs=blk,
        scratch_shapes=[pltpu.VMEM((LRU_TILE + SUBLANES, c), F32), pltpu.VMEM((LRU_TILE, c), F32),
                        pltpu.VMEM((LRU_TILE, c), F32), pltpu.VMEM((1, c), F32)],
        compiler_params=_params("parallel", "arbitrary"),
        name="rglru",
    )(xb.reshape(batch, seq, c), gb.reshape(batch, seq, c), conv_w, conv_b.reshape(1, c), wg, bg,
      lru_l.reshape(1, c), out_g.reshape(1, c))
    return out.reshape(batch * seq, c)


def _bias_kernel(rb_ref, onehot_ref, o_ref):
    o_ref[...] = jnp.dot(rb_ref[...], onehot_ref[...], preferred_element_type=F32,
                         precision=lax.Precision.HIGHEST)


def _t5_bucket(rel):
    n = jnp.maximum(rel, 0)
    max_exact = N_BUCKETS // 2
    nf = jnp.maximum(n, 1).astype(F32)
    large = max_exact + (jnp.log(nf / max_exact) / math.log(MAX_DISTANCE / max_exact)
                         * (N_BUCKETS - max_exact)).astype(jnp.int32)
    large = jnp.minimum(large, N_BUCKETS - 1)
    return jnp.where(n < max_exact, n, large)


def _band_bias(rel_bias):
    i = jnp.arange(BLOCK)[:, None]
    j = jnp.arange(2 * BLOCK)[None, :]
    bucket = _t5_bucket(BLOCK + i - j).reshape(1, -1)
    onehot = (bucket == jnp.arange(N_BUCKETS)[:, None]).astype(F32)
    out = pl.pallas_call(
        _bias_kernel,
        out_shape=jax.ShapeDtypeStruct((N_HEADS, BLOCK * 2 * BLOCK), F32),
        name="band_bias",
    )(rel_bias.astype(F32).T, onehot)
    return out.reshape(N_HEADS, BLOCK, 2 * BLOCK)


def _swa_kernel(sink_ref, q_ref, kc_ref, kp_ref, vc_ref, vp_ref, bias_ref, qg_ref, kg_ref, og_ref, o_ref):
    first_key = jnp.where(pl.program_id(1) > 0, 0, BLOCK)
    kk = jnp.concatenate([kp_ref[...], kc_ref[...]], axis=0)
    vv = jnp.concatenate([vp_ref[...], vc_ref[...]], axis=0).astype(BF16)
    shape = (GROUP * BLOCK, 2 * BLOCK)
    qi = lax.broadcasted_iota(jnp.int32, shape, 0) % BLOCK
    kj = lax.broadcasted_iota(jnp.int32, shape, 1)
    valid = (kj > jnp.maximum(qi + (BLOCK - WINDOW), first_key - 1)) & (kj <= qi + BLOCK)
    outs = []
    for hk in range(N_KV_HEADS):
        cols = slice(hk * HEAD_DIM, (hk + 1) * HEAD_DIM)
        k_n = _rms(kk[:, cols], kg_ref[...]).astype(BF16)
        qs, sinks = [], []
        for g in range(GROUP):
            h = hk * GROUP + g
            qs.append(_rms(q_ref[:, h * HEAD_DIM:(h + 1) * HEAD_DIM], qg_ref[...]).astype(BF16))
            sinks.append(jnp.full((BLOCK, 1), sink_ref[h], F32))
        q_n = jnp.concatenate(qs, axis=0)
        sink = jnp.concatenate(sinks, axis=0)
        s = lax.dot_general(q_n, k_n, (((1,), (1,)), ((), ())), preferred_element_type=F32)
        s = s * SCALE + bias_ref[hk * GROUP:(hk + 1) * GROUP].reshape(shape)
        s = jnp.where(valid, s, NEG_INF)
        m = jnp.maximum(jnp.max(s, axis=-1, keepdims=True), sink)
        p = jnp.exp(s - m)
        denom = jnp.sum(p, axis=-1, keepdims=True) + jnp.exp(sink - m)
        o = jnp.dot(p.astype(BF16), vv[:, cols], preferred_element_type=F32) / denom
        outs.extend(o[g * BLOCK:(g + 1) * BLOCK] for g in range(GROUP))
    y = jnp.concatenate(outs, axis=-1)
    o_ref[...] = _rms(y, og_ref[...]).astype(o_ref.dtype)


def _swa(q, k, v, bias, q_norm_g, k_norm_g, sinks, out_g, batch, seq):
    nb = seq // BLOCK
    cur = lambda w: pl.BlockSpec((None, BLOCK, w), lambda b, n: (b, n, 0))
    prev = lambda w: pl.BlockSpec((None, BLOCK, w), lambda b, n: (b, jnp.maximum(n - 1, 0), 0))
    k3 = k.reshape(batch, seq, KV_WIDTH)
    v3 = v.reshape(batch, seq, KV_WIDTH)
    out = pl.pallas_call(
        _swa_kernel,
        out_shape=jax.ShapeDtypeStruct((batch, seq, ATTN_WIDTH), BF16),
        grid=(batch, nb),
        in_specs=[pl.BlockSpec(memory_space=pltpu.SMEM), cur(ATTN_WIDTH), cur(KV_WIDTH), prev(KV_WIDTH),
                  cur(KV_WIDTH), prev(KV_WIDTH), _resident((N_HEADS, BLOCK, 2 * BLOCK)),
                  _resident((1, HEAD_DIM)), _resident((1, HEAD_DIM)), _resident((1, ATTN_WIDTH))],
        out_specs=cur(ATTN_WIDTH),
        compiler_params=_params("parallel", "parallel"),
        name="swa",
    )(sinks.astype(F32), q.reshape(batch, seq, ATTN_WIDTH), k3, k3, v3, v3, bias,
      q_norm_g.reshape(1, HEAD_DIM), k_norm_g.reshape(1, HEAD_DIM), out_g.reshape(1, ATTN_WIDTH))
    return out.reshape(batch * seq, ATTN_WIDTH)


def _outproj_kernel(x_ref, ml_ref, ma_ref, w_ref, g_ref, x1_ref, xn_ref):
    acc = jnp.dot(ml_ref[...], w_ref[:LRU_WIDTH, :], preferred_element_type=F32)
    acc = acc + jnp.dot(ma_ref[...], w_ref[LRU_WIDTH:, :], preferred_element_type=F32)
    x1 = x_ref[...] + acc
    x1_ref[...] = x1
    xn_ref[...] = _rms(x1, g_ref[...])


def _outproj(x2, y_lru, y_att, w_out, g):
    t = x2.shape[0]
    row = lambda w: pl.BlockSpec((ROW_TILE, w), lambda i: (i, 0))
    return pl.pallas_call(
        _outproj_kernel,
        out_shape=[jax.ShapeDtypeStruct((t, D_MODEL), F32)] * 2,
        grid=(t // ROW_TILE,),
        in_specs=[row(D_MODEL), row(LRU_WIDTH), row(ATTN_WIDTH), _resident((D_MODEL, D_MODEL)),
                  _resident((1, D_MODEL))],
        out_specs=[row(D_MODEL)] * 2,
        compiler_params=_params("parallel"),
        name="outproj",
    )(x2, y_lru, y_att, w_out.astype(BF16), g.reshape(1, D_MODEL))


def _top_rows(s, k, payload=None):
    n = s.shape[0]
    rows = lax.broadcasted_iota(jnp.int32, s.shape, 0)
    vals, picks = [], []
    for _ in range(k):
        m = jnp.max(s, axis=0, keepdims=True)
        first = jnp.min(jnp.where(s == m, rows, n), axis=0, keepdims=True)
        hit = rows == first
        vals.append(m)
        if payload is None:
            picks.append(first)
        else:
            picks.append(jnp.max(jnp.where(hit, payload, -1), axis=0, keepdims=True))
        s = jnp.where(hit, -jnp.inf, s)
    return jnp.concatenate(vals, axis=0), jnp.concatenate(picks, axis=0)


def _peer_topk_kernel(xn_ref, wq_ref, keys_ref, off_ref, g_ref, q_scr, off_scr, g_scr):
    q_t = lax.dot_general(wq_ref[...], xn_ref[...].astype(BF16), (((1,), (1,)), ((), ())),
                          preferred_element_type=F32)
    q_scr[...] = q_t.astype(BF16)

    def head(h, carry):
        tops = []
        for half in range(2):
            hc = h * 2 + half
            q_hc = q_scr[pl.ds(pl.multiple_of(hc * D_HALF, D_HALF), D_HALF), :]
            s = jnp.dot(keys_ref[hc], q_hc, preferred_element_type=F32)
            tops.append(_top_rows(s, TOPK))
        (s0, i0), (s1, i1) = tops
        cand = jnp.concatenate([s0[a:a + 1] + s1 for a in range(TOPK)], axis=0)
        cand_idx = jnp.concatenate([i0[a:a + 1] * N_KEYS + i1 for a in range(TOPK)], axis=0)
        best, idx = _top_rows(cand, TOPK, payload=cand_idx)
        e = jnp.exp(best - best[0:1])
        rows = pl.ds(pl.multiple_of(h * TOPK, TOPK), TOPK)
        off_scr[rows, :] = idx * WORD_ROWS
        g_scr[rows, :] = e / jnp.sum(e, axis=0, keepdims=True)
        return carry

    lax.fori_loop(0, PEER_HEADS, head, 0)
    off_ref[...] = off_scr[...].T
    g_ref[...] = g_scr[...].T


def _peer_topk(xn, w_query, sub_keys):
    t = xn.shape[0]
    nq = PEER_HEADS * D_QUERY
    out_blk = pl.BlockSpec((TOPK_TILE, N_PAIR), lambda i: (i, 0))
    return pl.pallas_call(
        _peer_topk_kernel,
        out_shape=[jax.ShapeDtypeStruct((t, N_PAIR), jnp.int32), jax.ShapeDtypeStruct((t, N_PAIR), F32)],
        grid=(t // TOPK_TILE,),
        in_specs=[pl.BlockSpec((TOPK_TILE, D_MODEL), lambda i: (i, 0)), _resident((nq, D_MODEL)),
                  _resident((2 * PEER_HEADS, N_KEYS, D_HALF))],
        out_specs=[out_blk, out_blk],
        scratch_shapes=[pltpu.VMEM((nq, TOPK_TILE), BF16), pltpu.VMEM((N_PAIR, TOPK_TILE), jnp.int32),
                        pltpu.VMEM((N_PAIR, TOPK_TILE), F32)],
        compiler_params=_params("parallel"),
        name="peer_topk",
    )(xn, w_query.T.astype(BF16), sub_keys.reshape(2 * PEER_HEADS, N_KEYS, D_HALF).astype(BF16))


def _pack_table(tab):
    n = tab.shape[0]
    t = tab.astype(BF16).reshape(n, WORD_ROWS, 2, LANES)
    return lax.bitcast_convert_type(jnp.swapaxes(t, -1, -2), jnp.int32).reshape(n * WORD_ROWS, LANES)


def _chunk_sum_matrix():
    return (np.arange(D_MODEL)[:, None] // N_CHUNK == np.arange(N_PAIR)[None, :]).astype(np.float32)


def _gather_rows(off_ref, tab_ref, t, buf):
    for k in range(N_PAIR):
        off = pl.multiple_of(off_ref[t, k], WORD_ROWS)
        buf[pl.ds(WORD_ROWS * k, WORD_ROWS), :] = tab_ref[pl.ds(off, WORD_ROWS), :]


def _gathered_rows(buf):
    return pltpu.bitcast(buf[...], BF16)


def _split_bf16(a):
    hi = a.astype(BF16)
    return jnp.concatenate([hi, (a - hi.astype(F32)).astype(BF16)], axis=0)


def _for_each_token(off_hbm, off_bufs, sem, tab_ref, row_bufs, compute):
    s = pl.program_id(0)

    def fetch(block, which):
        return pltpu.make_async_copy(off_hbm.at[block], off_bufs[which], sem.at[which])

    def consume(which):
        for t in range(PEER_HALF):
            buf = row_bufs[t % len(row_bufs)]
            _gather_rows(off_bufs[which], tab_ref, t, buf)
            compute(which * PEER_HALF + t, buf)

    @pl.when(s == 0)
    def _():
        fetch(0, 0).start()

    fetch(2 * s, 0).wait()
    fetch(2 * s + 1, 1).start()
    consume(0)
    fetch(2 * s + 1, 1).wait()

    @pl.when(s + 1 < pl.num_programs(0))
    def _():
        fetch(2 * s + 2, 0).start()

    consume(1)


def _group_row0():
    return pl.multiple_of((pl.program_id(0) % STEPS_PER_GROUP) * PEER_STEP, PEER_STEP)


def _peer_u_kernel(off_hbm, x_ref, tab_ref, sum_ref, gate_ref, w_ref, off_a, off_b, sem, buf_a, buf_b, z_scr):
    shape = (2 * N_CHUNK, D_MODEL)
    own = (lax.broadcasted_iota(jnp.int32, shape, 0) % N_CHUNK) == (lax.broadcasted_iota(jnp.int32, shape, 1) % N_CHUNK)
    row0 = _group_row0()

    def compute(t, buf):
        o = lax.dot_general(_split_bf16(x_ref[t]), _gathered_rows(buf), (((1,), (1,)), ((), ())),
                            preferred_element_type=F32)
        z_scr[pl.ds(row0 + t, 1), :] = jnp.sum(jnp.where(own, o, 0.0), axis=0, keepdims=True)

    _for_each_token(off_hbm, (off_a, off_b), sem, tab_ref, (buf_a, buf_b), compute)

    @pl.when(pl.program_id(0) % STEPS_PER_GROUP == STEPS_PER_GROUP - 1)
    def _():
        act = jnp.dot(z_scr[...], sum_ref[...], preferred_element_type=F32, precision=lax.Precision.HIGHEST)
        w_ref[...] = gate_ref[...] * _gelu(act)


def _peer_v_kernel(off_hbm, w_ref, x1_ref, tab_ref, spread_ref, y_ref, off_a, off_b, sem, buf_a, buf_b, wx_scr):
    shape = (N_CHUNK, D_MODEL)
    own = lax.broadcasted_iota(jnp.int32, shape, 0) == (lax.broadcasted_iota(jnp.int32, shape, 1) % N_CHUNK)
    row0 = _group_row0()

    @pl.when(pl.program_id(0) % STEPS_PER_GROUP == 0)
    def _():
        wx_scr[...] = jnp.dot(w_ref[...], spread_ref[...], preferred_element_type=F32,
                              precision=lax.Precision.HIGHEST)

    def compute(t, buf):
        lhs = jnp.where(own, jnp.broadcast_to(wx_scr[pl.ds(row0 + t, 1), :], shape), 0.0)
        r = jnp.dot(_split_bf16(lhs), _gathered_rows(buf), preferred_element_type=F32)
        y_ref[t] = x1_ref[t] + r[:N_CHUNK] + r[N_CHUNK:]

    _for_each_token(off_hbm, (off_a, off_b), sem, tab_ref, (buf_a, buf_b), compute)


def _peer_specs():
    pairs = pl.BlockSpec((PEER_GROUP, N_PAIR), lambda i: (i // STEPS_PER_GROUP, 0))
    rows = pl.BlockSpec((PEER_STEP, N_CHUNK, LANES), lambda i: (i, 0, 0))
    table = _resident((N_EXPERTS * WORD_ROWS, LANES))
    scratch = ([pltpu.SMEM((PEER_HALF, N_PAIR), jnp.int32)] * 2 + [pltpu.SemaphoreType.DMA((2,))]
               + [pltpu.VMEM((N_PAIR * WORD_ROWS, LANES), jnp.int32)] * 2 + [pltpu.VMEM((PEER_GROUP, D_MODEL), F32)])
    return pl.BlockSpec(memory_space=pl.ANY), pairs, rows, table, scratch


def _peer_u(off, gate, xn, tab_u):
    t = off.shape[0]
    off_s, pairs, rows, table, scratch = _peer_specs()
    return pl.pallas_call(
        _peer_u_kernel,
        out_shape=jax.ShapeDtypeStruct((t, N_PAIR), F32),
        grid=(t // PEER_STEP,),
        in_specs=[off_s, rows, table, _resident((D_MODEL, N_PAIR)), pairs],
        out_specs=pairs,
        scratch_shapes=scratch,
        compiler_params=_params("arbitrary"),
        name="peer_u",
    )(off.reshape(t // PEER_HALF, PEER_HALF, N_PAIR), xn.reshape(t, N_CHUNK, LANES), tab_u,
      jnp.asarray(_chunk_sum_matrix()), gate)


def _peer_v(off, w, x1, tab_v):
    t = off.shape[0]
    off_s, pairs, rows, table, scratch = _peer_specs()
    y = pl.pallas_call(
        _peer_v_kernel,
        out_shape=jax.ShapeDtypeStruct((t, N_CHUNK, LANES), F32),
        grid=(t // PEER_STEP,),
        in_specs=[off_s, pairs, rows, table, _resident((N_PAIR, D_MODEL))],
        out_specs=rows,
        scratch_shapes=scratch,
        compiler_params=_params("arbitrary"),
        name="peer_v",
    )(off.reshape(t // PEER_HALF, PEER_HALF, N_PAIR), w, x1.reshape(t, N_CHUNK, LANES), tab_v,
      jnp.asarray(_chunk_sum_matrix().T))
    return y.reshape(t, D_MODEL)


def kernel(x, ln_mix_g, w_in, conv_w, conv_b, w_gate_a, b_gate_a, w_gate_x, b_gate_x, lru_L, q_norm_g,
           k_norm_g, sinks, lru_out_g, attn_out_g, w_out, ln_ffn_g, w_query, sub_keys, expert_u, expert_v,
           rel_bias):
    batch, seq, _ = x.shape
    bias = _band_bias(rel_bias)
    x2 = x.reshape(batch * seq, D_MODEL)
    for l in range(w_in.shape[0]):
        xb, gb, q, k, v = _inproj(x2, ln_mix_g[l], w_in[l])
        y_lru = _rglru(xb, gb, conv_w[l], conv_b[l], w_gate_a[l], b_gate_a[l], w_gate_x[l], b_gate_x[l],
                       lru_L[l], lru_out_g[l], batch, seq)
        y_att = _swa(q, k, v, bias, q_norm_g[l], k_norm_g[l], sinks[l], attn_out_g[l], batch, seq)
        x1, xn = _outproj(x2, y_lru, y_att, w_out[l], ln_ffn_g[l])
        off, gate = _peer_topk(xn, w_query[l], sub_keys[l])
        w = _peer_u(off, gate, xn, _pack_table(expert_u[l]))
        x2 = _peer_v(off, w, x1, _pack_table(expert_v[l]))
    return x2.reshape(batch, seq, D_MODEL)
```

```python
import math

import jax
import jax.numpy as jnp
import numpy as np
from jax import lax
from jax.experimental import pallas as pl
from jax.experimental.pallas import tpu as pltpu

D_MODEL = 1024
LRU_WIDTH = 512
LRU_BLOCKS = 8
LRU_BLOCK = LRU_WIDTH // LRU_BLOCKS
CONV_WIDTH = 4
LRU_C = 8.0
N_HEADS = 8
N_KV_HEADS = 2
GROUP = N_HEADS // N_KV_HEADS
HEAD_DIM = 64
ATTN_WIDTH = N_HEADS * HEAD_DIM
KV_WIDTH = N_KV_HEADS * HEAD_DIM
WINDOW = 128
BLOCK = 128
N_BUCKETS = 32
MAX_DISTANCE = 128
PEER_HEADS = 8
N_KEYS = 128
N_EXPERTS = N_KEYS * N_KEYS
D_QUERY = 256
D_HALF = D_QUERY // 2
TOPK = 16
N_PAIR = PEER_HEADS * TOPK
IN_COLS = 2 * LRU_WIDTH + ATTN_WIDTH + 2 * KV_WIDTH
EPS = 1e-6
NEG_INF = -1e30
SCALE = HEAD_DIM ** -0.5

F32 = jnp.float32
BF16 = jnp.bfloat16
PACKED = jnp.uint32
LANES = 128
SUBLANES = 8
N_CHUNK = D_MODEL // LANES
WORD_ROWS = N_CHUNK // 2
VMEM_LIMIT = 52 * 1024 * 1024

ROW_TILE = 512
LRU_TILE = 256
TOPK_TILE = 256
PEER_HALF = 32
PEER_STEP = 2 * PEER_HALF
PEER_GROUP = 256
STEPS_PER_GROUP = PEER_GROUP // PEER_STEP


def _rms(x, g):
    return x * lax.rsqrt(jnp.mean(x * x, axis=-1, keepdims=True) + EPS) * g


def _gelu(x):
    return 0.5 * x * (1.0 + jnp.tanh(math.sqrt(2.0 / math.pi) * (x + 0.044715 * (x * x * x))))


def _params(*sem):
    return pltpu.CompilerParams(dimension_semantics=sem, vmem_limit_bytes=VMEM_LIMIT)


def _resident(shape):
    zeros = (0,) * len(shape)
    return pl.BlockSpec(shape, lambda *_: zeros, pipeline_mode=pl.Buffered(1))


def _inproj_kernel(x_ref, g_ref, w_ref, xb_ref, gb_ref, q_ref, k_ref, v_ref):
    h = _rms(x_ref[...], g_ref[...])
    p = jnp.dot(h.astype(BF16), w_ref[...], preferred_element_type=F32)
    c0, c1, c2, c3 = LRU_WIDTH, 2 * LRU_WIDTH, 2 * LRU_WIDTH + ATTN_WIDTH, IN_COLS - KV_WIDTH
    xb_ref[...] = p[:, :c0]
    gb_ref[...] = p[:, c0:c1]
    q_ref[...] = p[:, c1:c2]
    k_ref[...] = p[:, c2:c3]
    v_ref[...] = p[:, c3:]


def _inproj(x2, g, w_in):
    t = x2.shape[0]
    widths = (LRU_WIDTH, LRU_WIDTH, ATTN_WIDTH, KV_WIDTH, KV_WIDTH)
    row = lambda w: pl.BlockSpec((ROW_TILE, w), lambda i: (i, 0))
    return pl.pallas_call(
        _inproj_kernel,
        out_shape=[jax.ShapeDtypeStruct((t, w), F32) for w in widths],
        grid=(t // ROW_TILE,),
        in_specs=[row(D_MODEL), _resident((1, D_MODEL)), _resident((D_MODEL, IN_COLS))],
        out_specs=[row(w) for w in widths],
        compiler_params=_params("parallel"),
        name="inproj",
    )(x2, g.reshape(1, D_MODEL), w_in.astype(BF16))


def _rglru_kernel(xb_ref, gb_ref, cw_ref, cb_ref, wg_ref, bg_ref, l_ref, og_ref, o_ref,
                  xs_scr, a_scr, b_scr, h_scr):
    tt, c = xb_ref.shape

    @pl.when(pl.program_id(1) == 0)
    def _():
        xs_scr[0:SUBLANES, :] = jnp.zeros((SUBLANES, c), F32)
        h_scr[...] = jnp.zeros_like(h_scr)

    xb = xb_ref[...]
    xs_scr[SUBLANES:SUBLANES + tt, :] = xb
    xc = cb_ref[...] + xb * cw_ref[CONV_WIDTH - 1:CONV_WIDTH, :]
    for back in range(1, CONV_WIDTH):
        tap = CONV_WIDTH - 1 - back
        xc = xc + xs_scr[SUBLANES - back:SUBLANES - back + tt, :] * cw_ref[tap:tap + 1, :]
    xs_scr[0:SUBLANES, :] = xb[tt - SUBLANES:, :]

    gates = jnp.dot(xc.astype(BF16), wg_ref[...], preferred_element_type=F32) + bg_ref[...]
    r = jax.nn.sigmoid(gates[:, :c])
    ig = jax.nn.sigmoid(gates[:, c:])
    lam = l_ref[...]
    softplus_neg = jnp.maximum(-lam, 0.0) + jnp.log1p(jnp.exp(-jnp.abs(lam)))
    log_a = (-LRU_C) * r * softplus_neg
    a = jnp.exp(log_a)
    b = jnp.sqrt(-jnp.tanh(log_a) * (a * a + 1.0)) * (ig * xc)
    a_scr[...] = a
    b_scr[...] = b

    rows = lax.broadcasted_iota(jnp.int32, (SUBLANES, c), 0)

    def tile(i, h):
        off = pl.multiple_of(i * SUBLANES, SUBLANES)
        at = a_scr[pl.ds(off, SUBLANES), :]
        bt = b_scr[pl.ds(off, SUBLANES), :]
        for d in (1, 2, 4):
            keep = rows >= d
            bt = jnp.where(keep, at * pltpu.roll(bt, d, axis=0) + bt, bt)
            at = jnp.where(keep, at * pltpu.roll(at, d, axis=0), at)
        ht = at * h + bt
        b_scr[pl.ds(off, SUBLANES), :] = ht
        return ht[SUBLANES - 1:SUBLANES, :]

    h_scr[...] = lax.fori_loop(0, tt // SUBLANES, tile, h_scr[...])
    y = b_scr[...] * _gelu(gb_ref[...])
    o_ref[...] = _rms(y, og_ref[...]).astype(o_ref.dtype)


def _rglru(xb, gb, conv_w, conv_b, w_gate_a, b_gate_a, w_gate_x, b_gate_x, lru_l, out_g, batch, seq):
    c = LRU_WIDTH
    eye = jnp.eye(LRU_BLOCKS, dtype=F32)
    dense = lambda w: jnp.einsum("nij,nm->nimj", w, eye).reshape(c, c)
    wg = jnp.concatenate([dense(w_gate_a), dense(w_gate_x)], axis=1).astype(BF16)
    bg = jnp.concatenate([b_gate_a.reshape(1, c), b_gate_x.reshape(1, c)], axis=1)
    blk = pl.BlockSpec((None, LRU_TILE, c), lambda b, j: (b, j, 0))
    out = pl.pallas_call(
        _rglru_kernel,
        out_shape=jax.ShapeDtypeStruct((batch, seq, c), BF16),
        grid=(batch, seq // LRU_TILE),
        in_specs=[blk, blk, _resident((CONV_WIDTH, c)), _resident((1, c)), _resident((c, 2 * c)),
                  _resident((1, 2 * c)), _resident((1, c)), _resident((1, c))],
        out_specs=blk,
        scratch_shapes=[pltpu.VMEM((LRU_TILE + SUBLANES, c), F32), pltpu.VMEM((LRU_TILE, c), F32),
                        pltpu.VMEM((LRU_TILE, c), F32), pltpu.VMEM((1, c), F32)],
        compiler_params=_params("parallel", "arbitrary"),
        name="rglru",
    )(xb.reshape(batch, seq, c), gb.reshape(batch, seq, c), conv_w, conv_b.reshape(1, c), wg, bg,
      lru_l.reshape(1, c), out_g.reshape(1, c))
    return out.reshape(batch * seq, c)


def _bias_kernel(rb_ref, onehot_ref, o_ref):
    o_ref[...] = jnp.dot(rb_ref[...], onehot_ref[...], preferred_element_type=F32,
                         precision=lax.Precision.HIGHEST)


def _t5_bucket(rel):
    n = jnp.maximum(rel, 0)
    max_exact = N_BUCKETS // 2
    nf = jnp.maximum(n, 1).astype(F32)
    large = max_exact + jnp.floor(jnp.log(nf / max_exact) / math.log(MAX_DISTANCE / max_exact)
                                  * (N_BUCKETS - max_exact)).astype(jnp.int32)
    large = jnp.minimum(large, N_BUCKETS - 1)
    return jnp.where(n < max_exact, n, large)


def _band_bias(rel_bias):
    i = jnp.arange(BLOCK)[:, None]
    j = jnp.arange(2 * BLOCK)[None, :]
    bucket = _t5_bucket(BLOCK + i - j).reshape(1, -1)
    onehot = (bucket == jnp.arange(N_BUCKETS)[:, None]).astype(F32)
    out = pl.pallas_call(
        _bias_kernel,
        out_shape=jax.ShapeDtypeStruct((N_HEADS, BLOCK * 2 * BLOCK), F32),
        name="band_bias",
    )(rel_bias.astype(F32).T, onehot)
    return out.reshape(N_HEADS, BLOCK, 2 * BLOCK)


def _swa_kernel(sink_ref, q_ref, kc_ref, kp_ref, vc_ref, vp_ref, bias_ref, qg_ref, kg_ref, og_ref, o_ref):
    first_key = jnp.where(pl.program_id(1) > 0, 0, BLOCK)
    kk = jnp.concatenate([kp_ref[...], kc_ref[...]], axis=0)
    vv = jnp.concatenate([vp_ref[...], vc_ref[...]], axis=0).astype(BF16)
    shape = (GROUP * BLOCK, 2 * BLOCK)
    qi = lax.broadcasted_iota(jnp.int32, shape, 0) % BLOCK
    kj = lax.broadcasted_iota(jnp.int32, shape, 1)
    valid = (kj > jnp.maximum(qi + (BLOCK - WINDOW), first_key - 1)) & (kj <= qi + BLOCK)
    outs = []
    for hk in range(N_KV_HEADS):
        cols = slice(hk * HEAD_DIM, (hk + 1) * HEAD_DIM)
        k_n = _rms(kk[:, cols], kg_ref[...]).astype(BF16)
        qs, sinks = [], []
        for g in range(GROUP):
            h = hk * GROUP + g
            qs.append(_rms(q_ref[:, h * HEAD_DIM:(h + 1) * HEAD_DIM], qg_ref[...]).astype(BF16))
            sinks.append(jnp.full((BLOCK, 1), sink_ref[h], F32))
        q_n = jnp.concatenate(qs, axis=0)
        sink = jnp.concatenate(sinks, axis=0)
        s = lax.dot_general(q_n, k_n, (((1,), (1,)), ((), ())), preferred_element_type=F32)
        s = s * SCALE + bias_ref[hk * GROUP:(hk + 1) * GROUP].reshape(shape)
        s = jnp.where(valid, s, NEG_INF)
        m = jnp.maximum(jnp.max(s, axis=-1, keepdims=True), sink)
        p = jnp.exp(s - m)
        denom = jnp.sum(p, axis=-1, keepdims=True) + jnp.exp(sink - m)
        o = jnp.dot(p.astype(BF16), vv[:, cols], preferred_element_type=F32) / denom
        outs.extend(o[g * BLOCK:(g + 1) * BLOCK] for g in range(GROUP))
    y = jnp.concatenate(outs, axis=-1)
    o_ref[...] = _rms(y, og_ref[...]).astype(o_ref.dtype)


def _swa(q, k, v, bias, q_norm_g, k_norm_g, sinks, out_g, batch, seq):
    nb = seq // BLOCK
    cur = lambda w: pl.BlockSpec((None, BLOCK, w), lambda b, n: (b, n, 0))
    prev = lambda w: pl.BlockSpec((None, BLOCK, w), lambda b, n: (b, jnp.maximum(n - 1, 0), 0))
    k3 = k.reshape(batch, seq, KV_WIDTH)
    v3 = v.reshape(batch, seq, KV_WIDTH)
    out = pl.pallas_call(
        _swa_kernel,
        out_shape=jax.ShapeDtypeStruct((batch, seq, ATTN_WIDTH), BF16),
        grid=(batch, nb),
        in_specs=[pl.BlockSpec(memory_space=pltpu.SMEM), cur(ATTN_WIDTH), cur(KV_WIDTH), prev(KV_WIDTH),
                  cur(KV_WIDTH), prev(KV_WIDTH), _resident((N_HEADS, BLOCK, 2 * BLOCK)),
                  _resident((1, HEAD_DIM)), _resident((1, HEAD_DIM)), _resident((1, ATTN_WIDTH))],
        out_specs=cur(ATTN_WIDTH),
        compiler_params=_params("parallel", "parallel"),
        name="swa",
    )(sinks.astype(F32), q.reshape(batch, seq, ATTN_WIDTH), k3, k3, v3, v3, bias,
      q_norm_g.reshape(1, HEAD_DIM), k_norm_g.reshape(1, HEAD_DIM), out_g.reshape(1, ATTN_WIDTH))
    return out.reshape(batch * seq, ATTN_WIDTH)


def _outproj_kernel(x_ref, ml_ref, ma_ref, w_ref, g_ref, x1_ref, xn_ref):
    acc = jnp.dot(ml_ref[...], w_ref[:LRU_WIDTH, :], preferred_element_type=F32)
    acc = acc + jnp.dot(ma_ref[...], w_ref[LRU_WIDTH:, :], preferred_element_type=F32)
    x1 = x_ref[...] + acc
    x1_ref[...] = x1
    xn_ref[...] = _rms(x1, g_ref[...])


def _outproj(x2, y_lru, y_att, w_out, g):
    t = x2.shape[0]
    row = lambda w: pl.BlockSpec((ROW_TILE, w), lambda i: (i, 0))
    return pl.pallas_call(
        _outproj_kernel,
        out_shape=[jax.ShapeDtypeStruct((t, D_MODEL), F32)] * 2,
        grid=(t // ROW_TILE,),
        in_specs=[row(D_MODEL), row(LRU_WIDTH), row(ATTN_WIDTH), _resident((D_MODEL, D_MODEL)),
                  _resident((1, D_MODEL))],
        out_specs=[row(D_MODEL)] * 2,
        compiler_params=_params("parallel"),
        name="outproj",
    )(x2, y_lru, y_att, w_out.astype(BF16), g.reshape(1, D_MODEL))


def _pair_candidates(s0, i0, s1, i1):
    b_ids = lax.broadcasted_iota(jnp.int32, (SUBLANES,) + s1.shape[1:], 0)
    vals, ids = [], []
    a = 0
    while TOPK // (a + 1) > 1:
        n_b = TOPK // (a + 1)
        n_rows = -(-n_b // SUBLANES) * SUBLANES
        v = s0[a:a + 1] + s1[:n_rows]
        if n_b < n_rows:
            v = jnp.where(b_ids < n_b, v, -jnp.inf)
        vals.append(v)
        ids.append(i0[a:a + 1] * N_KEYS + i1[:n_rows])
        a += 1
    vals.append(s0[a:] + s1[0:1])
    ids.append(i0[a:] * N_KEYS + i1[0:1])
    return jnp.concatenate(vals, axis=0), jnp.concatenate(ids, axis=0)


def _top_rows(s, payload, k):
    n = s.shape[0]
    rows = lax.broadcasted_iota(jnp.int32, s.shape, 0)
    vals, picks = [], []
    for _ in range(k):
        m = jnp.max(s, axis=0, keepdims=True)
        first = jnp.min(jnp.where(s == m, rows, n), axis=0, keepdims=True)
        hit = rows == first
        vals.append(m)
        if payload is None:
            picks.append(first)
        else:
            picks.append(jnp.max(jnp.where(hit, payload, -1), axis=0, keepdims=True))
        s = jnp.where(hit, -jnp.inf, s)
    return jnp.concatenate(vals, axis=0), jnp.concatenate(picks, axis=0)


def _peer_topk_kernel(xn_ref, wq_ref, keys_ref, off_ref, g_ref, q_scr, off_scr, g_scr):
    q_t = lax.dot_general(wq_ref[...], xn_ref[...].astype(BF16), (((1,), (1,)), ((), ())),
                          preferred_element_type=F32)
    q_scr[...] = q_t.astype(BF16)

    def head(h, carry):
        tops = []
        for half in range(2):
            hc = h * 2 + half
            q_hc = q_scr[pl.ds(pl.multiple_of(hc * D_HALF, D_HALF), D_HALF), :]
            s = jnp.dot(keys_ref[hc], q_hc, preferred_element_type=F32)
            tops.append(_top_rows(s, None, TOPK))
        (s0, i0), (s1, i1) = tops
        best, idx = _top_rows(*_pair_candidates(s0, i0, s1, i1), TOPK)
        e = jnp.exp(best - best[0:1])
        rows = pl.ds(pl.multiple_of(h * TOPK, TOPK), TOPK)
        off_scr[rows, :] = idx * WORD_ROWS
        g_scr[rows, :] = e / jnp.sum(e, axis=0, keepdims=True)
        return carry

    lax.fori_loop(0, PEER_HEADS, head, 0)
    off_ref[...] = off_scr[...].T
    g_ref[...] = g_scr[...].T


def _peer_topk(xn, w_query, sub_keys):
    t = xn.shape[0]
    nq = PEER_HEADS * D_QUERY
    out_blk = pl.BlockSpec((TOPK_TILE, N_PAIR), lambda i: (i, 0))
    return pl.pallas_call(
        _peer_topk_kernel,
        out_shape=[jax.ShapeDtypeStruct((t, N_PAIR), jnp.int32), jax.ShapeDtypeStruct((t, N_PAIR), F32)],
        grid=(t // TOPK_TILE,),
        in_specs=[pl.BlockSpec((TOPK_TILE, D_MODEL), lambda i: (i, 0)), _resident((nq, D_MODEL)),
                  _resident((2 * PEER_HEADS, N_KEYS, D_HALF))],
        out_specs=[out_blk, out_blk],
        scratch_shapes=[pltpu.VMEM((nq, TOPK_TILE), BF16), pltpu.VMEM((N_PAIR, TOPK_TILE), jnp.int32),
                        pltpu.VMEM((N_PAIR, TOPK_TILE), F32)],
        compiler_params=_params("parallel"),
        name="peer_topk",
    )(xn, w_query.T.astype(BF16), sub_keys.reshape(2 * PEER_HEADS, N_KEYS, D_HALF).astype(BF16))


def _pack_kernel(x_ref, o_ref):
    rows = x_ref.shape[0]
    for j in range(WORD_ROWS):
        lo = x_ref[:, j * LANES:(j + 1) * LANES]
        hi = x_ref[:, (WORD_ROWS + j) * LANES:(WORD_ROWS + j + 1) * LANES]
        o_ref[pl.ds(j, rows, stride=WORD_ROWS), :] = pltpu.pack_elementwise([lo, hi], packed_dtype=BF16)


def _pack_table(tab):
    n = tab.shape[0]
    return pl.pallas_call(
        _pack_kernel,
        out_shape=jax.ShapeDtypeStruct((n * WORD_ROWS, LANES), PACKED),
        grid=(n // ROW_TILE,),
        in_specs=[pl.BlockSpec((ROW_TILE, D_MODEL), lambda i: (i, 0))],
        out_specs=pl.BlockSpec((ROW_TILE * WORD_ROWS, LANES), lambda i: (i, 0)),
        compiler_params=_params("parallel"),
        name="pack_table",
    )(tab)


def _slot_chunk(slot):
    return slot // 2 + WORD_ROWS * (slot % 2)


def _chunk_sum_matrix():
    return (np.arange(D_MODEL)[:, None] // N_CHUNK == np.arange(N_PAIR)[None, :]).astype(np.float32)


def _gather_rows(off_ref, tab_ref, t, buf):
    per_store = SUBLANES // WORD_ROWS
    for k in range(0, N_PAIR, per_store):
        parts = [tab_ref[pl.ds(pl.multiple_of(off_ref[t, k + i], WORD_ROWS), WORD_ROWS), :] for i in range(per_store)]
        buf[pl.ds(WORD_ROWS * k, SUBLANES), :] = jnp.concatenate(parts, axis=0)


def _gathered_rows(buf):
    return pltpu.bitcast(buf[...], BF16)


def _split_bf16(a):
    hi = a.astype(BF16)
    return jnp.concatenate([hi, (a - hi.astype(F32)).astype(BF16)], axis=0)


def _for_each_token(off_hbm, off_bufs, sem, tab_ref, row_bufs, compute):
    s = pl.program_id(0)

    def fetch(block, which):
        return pltpu.make_async_copy(off_hbm.at[block], off_bufs[which], sem.at[which])

    def consume(which):
        for t in range(PEER_HALF):
            buf = row_bufs[t % len(row_bufs)]
            _gather_rows(off_bufs[which], tab_ref, t, buf)
            compute(which * PEER_HALF + t, buf)

    @pl.when(s == 0)
    def _():
        fetch(0, 0).start()

    fetch(2 * s, 0).wait()
    fetch(2 * s + 1, 1).start()
    consume(0)
    fetch(2 * s + 1, 1).wait()

    @pl.when(s + 1 < pl.num_programs(0))
    def _():
        fetch(2 * s + 2, 0).start()

    consume(1)


def _group_row0():
    return pl.multiple_of((pl.program_id(0) % STEPS_PER_GROUP) * PEER_STEP, PEER_STEP)


def _peer_u_kernel(off_hbm, x_ref, tab_ref, sum_ref, gate_ref, w_ref, off_a, off_b, sem, buf_a, buf_b, z_scr):
    shape = (2 * N_CHUNK, D_MODEL)
    own = (lax.broadcasted_iota(jnp.int32, shape, 0) % N_CHUNK) == _slot_chunk(lax.broadcasted_iota(jnp.int32, shape, 1) % N_CHUNK)
    row0 = _group_row0()

    def compute(t, buf):
        o = lax.dot_general(_split_bf16(x_ref[t]), _gathered_rows(buf), (((1,), (1,)), ((), ())),
                            preferred_element_type=F32)
        z_scr[pl.ds(row0 + t, 1), :] = jnp.sum(jnp.where(own, o, 0.0), axis=0, keepdims=True)

    _for_each_token(off_hbm, (off_a, off_b), sem, tab_ref, (buf_a, buf_b), compute)

    @pl.when(pl.program_id(0) % STEPS_PER_GROUP == STEPS_PER_GROUP - 1)
    def _():
        act = jnp.dot(z_scr[...], sum_ref[...], preferred_element_type=F32, precision=lax.Precision.HIGHEST)
        w_ref[...] = gate_ref[...] * _gelu(act)


def _peer_v_kernel(off_hbm, w_ref, x1_ref, tab_ref, spread_ref, y_ref, off_a, off_b, sem, buf_a, buf_b, wx_scr):
    shape = (N_CHUNK, D_MODEL)
    own = lax.broadcasted_iota(jnp.int32, shape, 0) == _slot_chunk(lax.broadcasted_iota(jnp.int32, shape, 1) % N_CHUNK)
    row0 = _group_row0()

    @pl.when(pl.program_id(0) % STEPS_PER_GROUP == 0)
    def _():
        wx_scr[...] = jnp.dot(w_ref[...], spread_ref[...], preferred_element_type=F32,
                              precision=lax.Precision.HIGHEST)

    def compute(t, buf):
        lhs = jnp.where(own, jnp.broadcast_to(wx_scr[pl.ds(row0 + t, 1), :], shape), 0.0)
        r = jnp.dot(_split_bf16(lhs), _gathered_rows(buf), preferred_element_type=F32)
        y_ref[t] = x1_ref[t] + r[:N_CHUNK] + r[N_CHUNK:]

    _for_each_token(off_hbm, (off_a, off_b), sem, tab_ref, (buf_a, buf_b), compute)


def _peer_specs():
    pairs = pl.BlockSpec((PEER_GROUP, N_PAIR), lambda i: (i // STEPS_PER_GROUP, 0))
    rows = pl.BlockSpec((PEER_STEP, N_CHUNK, LANES), lambda i: (i, 0, 0))
    table = _resident((N_EXPERTS * WORD_ROWS, LANES))
    scratch = ([pltpu.SMEM((PEER_HALF, N_PAIR), jnp.int32)] * 2 + [pltpu.SemaphoreType.DMA((2,))]
               + [pltpu.VMEM((N_PAIR * WORD_ROWS, LANES), PACKED)] * 2 + [pltpu.VMEM((PEER_GROUP, D_MODEL), F32)])
    return pl.BlockSpec(memory_space=pl.ANY), pairs, rows, table, scratch


def _peer_u(off, gate, xn, tab_u):
    t = off.shape[0]
    off_s, pairs, rows, table, scratch = _peer_specs()
    return pl.pallas_call(
        _peer_u_kernel,
        out_shape=jax.ShapeDtypeStruct((t, N_PAIR), F32),
        grid=(t // PEER_STEP,),
        in_specs=[off_s, rows, table, _resident((D_MODEL, N_PAIR)), pairs],
        out_specs=pairs,
        scratch_shapes=scratch,
        compiler_params=_params("arbitrary"),
        name="peer_u",
    )(off.reshape(t // PEER_HALF, PEER_HALF, N_PAIR), xn.reshape(t, N_CHUNK, LANES), tab_u,
      jnp.asarray(_chunk_sum_matrix()), gate)


def _peer_v(off, w, x1, tab_v):
    t = off.shape[0]
    off_s, pairs, rows, table, scratch = _peer_specs()
    y = pl.pallas_call(
        _peer_v_kernel,
        out_shape=jax.ShapeDtypeStruct((t, N_CHUNK, LANES), F32),
        grid=(t // PEER_STEP,),
        in_specs=[off_s, pairs, rows, table, _resident((N_PAIR, D_MODEL))],
        out_specs=rows,
        scratch_shapes=scratch,
        compiler_params=_params("arbitrary"),
        name="peer_v",
    )(off.reshape(t // PEER_HALF, PEER_HALF, N_PAIR), w, x1.reshape(t, N_CHUNK, LANES), tab_v,
      jnp.asarray(_chunk_sum_matrix().T))
    return y.reshape(t, D_MODEL)


def kernel(x, ln_mix_g, w_in, conv_w, conv_b, w_gate_a, b_gate_a, w_gate_x, b_gate_x, lru_L, q_norm_g,
           k_norm_g, sinks, lru_out_g, attn_out_g, w_out, ln_ffn_g, w_query, sub_keys, expert_u, expert_v,
           rel_bias):
    batch, seq, _ = x.shape
    bias = _band_bias(rel_bias)
    x2 = x.reshape(batch * seq, D_MODEL)
    for l in range(w_in.shape[0]):
        xb, gb, q, k, v = _inproj(x2, ln_mix_g[l], w_in[l])
        y_lru = _rglru(xb, gb, conv_w[l], conv_b[l], w_gate_a[l], b_gate_a[l], w_gate_x[l], b_gate_x[l],
                       lru_L[l], lru_out_g[l], batch, seq)
        y_att = _swa(q, k, v, bias, q_norm_g[l], k_norm_g[l], sinks[l], attn_out_g[l], batch, seq)
        x1, xn = _outproj(x2, y_lru, y_att, w_out[l], ln_ffn_g[l])
        off, gate = _peer_topk(xn, w_query[l], sub_keys[l])
        w = _peer_u(off, gate, xn, _pack_table(expert_u[l]))
        x2 = _peer_v(off, w, x1, _pack_table(expert_v[l]))
    return x2.reshape(batch, seq, D_MODEL)
```

```python
import math

import jax
import jax.numpy as jnp
import numpy as np
from jax import lax
from jax.experimental import pallas as pl
from jax.experimental.pallas import tpu as pltpu

D_MODEL = 1024
LRU_WIDTH = 512
LRU_BLOCKS = 8
LRU_BLOCK = LRU_WIDTH // LRU_BLOCKS
CONV_WIDTH = 4
LRU_C = 8.0
N_HEADS = 8
N_KV_HEADS = 2
GROUP = N_HEADS // N_KV_HEADS
HEAD_DIM = 64
ATTN_WIDTH = N_HEADS * HEAD_DIM
KV_WIDTH = N_KV_HEADS * HEAD_DIM
WINDOW = 128
BLOCK = 128
N_BUCKETS = 32
MAX_DISTANCE = 128
PEER_HEADS = 8
N_KEYS = 128
N_EXPERTS = N_KEYS * N_KEYS
D_QUERY = 256
D_HALF = D_QUERY // 2
TOPK = 16
N_PAIR = PEER_HEADS * TOPK
IN_COLS = 2 * LRU_WIDTH + ATTN_WIDTH + 2 * KV_WIDTH
EPS = 1e-6
NEG_INF = -1e30
SCALE = HEAD_DIM ** -0.5

F32 = jnp.float32
BF16 = jnp.bfloat16
PACKED = jnp.uint32
LANES = 128
SUBLANES = 8
N_CHUNK = D_MODEL // LANES
WORD_ROWS = N_CHUNK // 2
MXU_ROWS = 256
TILE_PAIRS = MXU_ROWS // N_CHUNK
N_TILES = N_PAIR // TILE_PAIRS
VMEM_LIMIT = 52 * 1024 * 1024

ROW_TILE = 512
LRU_TILE = 256
TOPK_TILE = 256
PEER_HALF = 64
PEER_STEP = 2 * PEER_HALF
PEER_GROUP = 256
STEPS_PER_GROUP = PEER_GROUP // PEER_STEP


def _rms(x, g):
    return x * lax.rsqrt(jnp.mean(x * x, axis=-1, keepdims=True) + EPS) * g


def _gelu(x):
    return 0.5 * x * (1.0 + jnp.tanh(math.sqrt(2.0 / math.pi) * (x + 0.044715 * (x * x * x))))


def _params(*sem):
    return pltpu.CompilerParams(dimension_semantics=sem, vmem_limit_bytes=VMEM_LIMIT)


def _resident(shape):
    zeros = (0,) * len(shape)
    return pl.BlockSpec(shape, lambda *_: zeros, pipeline_mode=pl.Buffered(1))


def _inproj_kernel(x_ref, g_ref, w_ref, xb_ref, gb_ref, q_ref, k_ref, v_ref):
    h = _rms(x_ref[...], g_ref[...])
    p = jnp.dot(h.astype(BF16), w_ref[...], preferred_element_type=F32)
    c0, c1, c2, c3 = LRU_WIDTH, 2 * LRU_WIDTH, 2 * LRU_WIDTH + ATTN_WIDTH, IN_COLS - KV_WIDTH
    xb_ref[...] = p[:, :c0]
    gb_ref[...] = p[:, c0:c1]
    q_ref[...] = p[:, c1:c2]
    k_ref[...] = p[:, c2:c3]
    v_ref[...] = p[:, c3:]


def _inproj(x2, g, w_in):
    t = x2.shape[0]
    widths = (LRU_WIDTH, LRU_WIDTH, ATTN_WIDTH, KV_WIDTH, KV_WIDTH)
    row = lambda w: pl.BlockSpec((ROW_TILE, w), lambda i: (i, 0))
    return pl.pallas_call(
        _inproj_kernel,
        out_shape=[jax.ShapeDtypeStruct((t, w), F32) for w in widths],
        grid=(t // ROW_TILE,),
        in_specs=[row(D_MODEL), _resident((1, D_MODEL)), _resident((D_MODEL, IN_COLS))],
        out_specs=[row(w) for w in widths],
        compiler_params=_params("parallel"),
        name="inproj",
    )(x2, g.reshape(1, D_MODEL), w_in.astype(BF16))


def _rglru_kernel(xb_ref, gb_ref, cw_ref, cb_ref, wg_ref, bg_ref, l_ref, og_ref, o_ref,
                  xs_scr, a_scr, b_scr, h_scr):
    tt, c = xb_ref.shape

    @pl.when(pl.program_id(1) == 0)
    def _():
        xs_scr[0:SUBLANES, :] = jnp.zeros((SUBLANES, c), F32)
        h_scr[...] = jnp.zeros_like(h_scr)

    xb = xb_ref[...]
    xs_scr[SUBLANES:SUBLANES + tt, :] = xb
    xc = cb_ref[...] + xb * cw_ref[CONV_WIDTH - 1:CONV_WIDTH, :]
    for back in range(1, CONV_WIDTH):
        tap = CONV_WIDTH - 1 - back
        xc = xc + xs_scr[SUBLANES - back:SUBLANES - back + tt, :] * cw_ref[tap:tap + 1, :]
    xs_scr[0:SUBLANES, :] = xb[tt - SUBLANES:, :]

    gates = jnp.dot(xc.astype(BF16), wg_ref[...], preferred_element_type=F32) + bg_ref[...]
    r = jax.nn.sigmoid(gates[:, :c])
    ig = jax.nn.sigmoid(gates[:, c:])
    lam = l_ref[...]
    softplus_neg = jnp.maximum(-lam, 0.0) + jnp.log1p(jnp.exp(-jnp.abs(lam)))
    log_a = (-LRU_C) * r * softplus_neg
    a = jnp.exp(log_a)
    b = jnp.sqrt(-jnp.tanh(log_a) * (a * a + 1.0)) * (ig * xc)
    a_scr[...] = a
    b_scr[...] = b

    rows = lax.broadcasted_iota(jnp.int32, (SUBLANES, c), 0)

    def tile(i, h):
        off = pl.multiple_of(i * SUBLANES, SUBLANES)
        at = a_scr[pl.ds(off, SUBLANES), :]
        bt = b_scr[pl.ds(off, SUBLANES), :]
        for d in (1, 2, 4):
            keep = rows >= d
            bt = jnp.where(keep, at * pltpu.roll(bt, d, axis=0) + bt, bt)
            at = jnp.where(keep, at * pltpu.roll(at, d, axis=0), at)
        ht = at * h + bt
        b_scr[pl.ds(off, SUBLANES), :] = ht
        return ht[SUBLANES - 1:SUBLANES, :]

    h_scr[...] = lax.fori_loop(0, tt // SUBLANES, tile, h_scr[...])
    y = b_scr[...] * _gelu(gb_ref[...])
    o_ref[...] = _rms(y, og_ref[...]).astype(o_ref.dtype)


def _rglru(xb, gb, conv_w, conv_b, w_gate_a, b_gate_a, w_gate_x, b_gate_x, lru_l, out_g, batch, seq):
    c = LRU_WIDTH
    eye = jnp.eye(LRU_BLOCKS, dtype=F32)
    dense = lambda w: jnp.einsum("nij,nm->nimj", w, eye).reshape(c, c)
    wg = jnp.concatenate([dense(w_gate_a), dense(w_gate_x)], axis=1).astype(BF16)
    bg = jnp.concatenate([b_gate_a.reshape(1, c), b_gate_x.reshape(1, c)], axis=1)
    blk = pl.BlockSpec((None, LRU_TILE, c), lambda b, j: (b, j, 0))
    out = pl.pallas_call(
        _rglru_kernel,
        out_shape=jax.ShapeDtypeStruct((batch, seq, c), BF16),
        grid=(batch, seq // LRU_TILE),
        in_specs=[blk, blk, _resident((CONV_WIDTH, c)), _resident((1, c)), _resident((c, 2 * c)),
                  _resident((1, 2 * c)), _resident((1, c)), _resident((1, c))],
        out_specs=blk,
        scratch_shapes=[pltpu.VMEM((LRU_TILE + SUBLANES, c), F32), pltpu.VMEM((LRU_TILE, c), F32),
                        pltpu.VMEM((LRU_TILE, c), F32), pltpu.VMEM((1, c), F32)],
        compiler_params=_params("parallel", "arbitrary"),
        name="rglru",
    )(xb.reshape(batch, seq, c), gb.reshape(batch, seq, c), conv_w, conv_b.reshape(1, c), wg, bg,
      lru_l.reshape(1, c), out_g.reshape(1, c))
    return out.reshape(batch * seq, c)


def _bias_kernel(rb_ref, onehot_ref, o_ref):
    o_ref[...] = jnp.dot(rb_ref[...], onehot_ref[...], preferred_element_type=F32,
                         precision=lax.Precision.HIGHEST)


def _t5_bucket(rel):
    n = jnp.maximum(rel, 0)
    max_exact = N_BUCKETS // 2
    nf = jnp.maximum(n, 1).astype(F32)
    large = max_exact + jnp.floor(jnp.log(nf / max_exact) / math.log(MAX_DISTANCE / max_exact)
                                  * (N_BUCKETS - max_exact)).astype(jnp.int32)
    large = jnp.minimum(large, N_BUCKETS - 1)
    return jnp.where(n < max_exact, n, large)


def _band_bias(rel_bias):
    i = jnp.arange(BLOCK)[:, None]
    j = jnp.arange(2 * BLOCK)[None, :]
    bucket = _t5_bucket(BLOCK + i - j).reshape(1, -1)
    onehot = (bucket == jnp.arange(N_BUCKETS)[:, None]).astype(F32)
    out = pl.pallas_call(
        _bias_kernel,
        out_shape=jax.ShapeDtypeStruct((N_HEADS, BLOCK * 2 * BLOCK), F32),
        name="band_bias",
    )(rel_bias.astype(F32).T, onehot)
    return out.reshape(N_HEADS, BLOCK, 2 * BLOCK)


def _swa_kernel(sink_ref, q_ref, kc_ref, kp_ref, vc_ref, vp_ref, bias_ref, qg_ref, kg_ref, og_ref, o_ref):
    first_key = jnp.where(pl.program_id(1) > 0, 0, BLOCK)
    kk = jnp.concatenate([kp_ref[...], kc_ref[...]], axis=0)
    vv = jnp.concatenate([vp_ref[...], vc_ref[...]], axis=0).astype(BF16)
    shape = (GROUP * BLOCK, 2 * BLOCK)
    qi = lax.broadcasted_iota(jnp.int32, shape, 0) % BLOCK
    kj = lax.broadcasted_iota(jnp.int32, shape, 1)
    valid = (kj > jnp.maximum(qi + (BLOCK - WINDOW), first_key - 1)) & (kj <= qi + BLOCK)
    outs = []
    for hk in range(N_KV_HEADS):
        cols = slice(hk * HEAD_DIM, (hk + 1) * HEAD_DIM)
        k_n = _rms(kk[:, cols], kg_ref[...]).astype(BF16)
        qs, sinks = [], []
        for g in range(GROUP):
            h = hk * GROUP + g
            qs.append(_rms(q_ref[:, h * HEAD_DIM:(h + 1) * HEAD_DIM], qg_ref[...]).astype(BF16))
            sinks.append(jnp.full((BLOCK, 1), sink_ref[h], F32))
        q_n = jnp.concatenate(qs, axis=0)
        sink = jnp.concatenate(sinks, axis=0)
        s = lax.dot_general(q_n, k_n, (((1,), (1,)), ((), ())), preferred_element_type=F32)
        s = s * SCALE + bias_ref[hk * GROUP:(hk + 1) * GROUP].reshape(shape)
        s = jnp.where(valid, s, NEG_INF)
        m = jnp.maximum(jnp.max(s, axis=-1, keepdims=True), sink)
        p = jnp.exp(s - m)
        denom = jnp.sum(p, axis=-1, keepdims=True) + jnp.exp(sink - m)
        o = jnp.dot(p.astype(BF16), vv[:, cols], preferred_element_type=F32) / denom
        outs.extend(o[g * BLOCK:(g + 1) * BLOCK] for g in range(GROUP))
    y = jnp.concatenate(outs, axis=-1)
    o_ref[...] = _rms(y, og_ref[...]).astype(o_ref.dtype)


def _swa(q, k, v, bias, q_norm_g, k_norm_g, sinks, out_g, batch, seq):
    nb = seq // BLOCK
    cur = lambda w: pl.BlockSpec((None, BLOCK, w), lambda b, n: (b, n, 0))
    prev = lambda w: pl.BlockSpec((None, BLOCK, w), lambda b, n: (b, jnp.maximum(n - 1, 0), 0))
    k3 = k.reshape(batch, seq, KV_WIDTH)
    v3 = v.reshape(batch, seq, KV_WIDTH)
    out = pl.pallas_call(
        _swa_kernel,
        out_shape=jax.ShapeDtypeStruct((batch, seq, ATTN_WIDTH), BF16),
        grid=(batch, nb),
        in_specs=[pl.BlockSpec(memory_space=pltpu.SMEM), cur(ATTN_WIDTH), cur(KV_WIDTH), prev(KV_WIDTH),
                  cur(KV_WIDTH), prev(KV_WIDTH), _resident((N_HEADS, BLOCK, 2 * BLOCK)),
                  _resident((1, HEAD_DIM)), _resident((1, HEAD_DIM)), _resident((1, ATTN_WIDTH))],
        out_specs=cur(ATTN_WIDTH),
        compiler_params=_params("parallel", "parallel"),
        name="swa",
    )(sinks.astype(F32), q.reshape(batch, seq, ATTN_WIDTH), k3, k3, v3, v3, bias,
      q_norm_g.reshape(1, HEAD_DIM), k_norm_g.reshape(1, HEAD_DIM), out_g.reshape(1, ATTN_WIDTH))
    return out.reshape(batch * seq, ATTN_WIDTH)


def _outproj_kernel(x_ref, ml_ref, ma_ref, w_ref, g_ref, x1_ref, xn_ref):
    acc = jnp.dot(ml_ref[...], w_ref[:LRU_WIDTH, :], preferred_element_type=F32)
    acc = acc + jnp.dot(ma_ref[...], w_ref[LRU_WIDTH:, :], preferred_element_type=F32)
    x1 = x_ref[...] + acc
    x1_ref[...] = x1
    xn_ref[...] = _rms(x1, g_ref[...])


def _outproj(x2, y_lru, y_att, w_out, g):
    t = x2.shape[0]
    row = lambda w: pl.BlockSpec((ROW_TILE, w), lambda i: (i, 0))
    return pl.pallas_call(
        _outproj_kernel,
        out_shape=[jax.ShapeDtypeStruct((t, D_MODEL), F32)] * 2,
        grid=(t // ROW_TILE,),
        in_specs=[row(D_MODEL), row(LRU_WIDTH), row(ATTN_WIDTH), _resident((D_MODEL, D_MODEL)),
                  _resident((1, D_MODEL))],
        out_specs=[row(D_MODEL)] * 2,
        compiler_params=_params("parallel"),
        name="outproj",
    )(x2, y_lru, y_att, w_out.astype(BF16), g.reshape(1, D_MODEL))


def _pair_candidates(s0, i0, s1, i1):
    b_ids = lax.broadcasted_iota(jnp.int32, (SUBLANES,) + s1.shape[1:], 0)
    vals, ids = [], []
    a = 0
    while TOPK // (a + 1) > 1:
        n_b = TOPK // (a + 1)
        n_rows = -(-n_b // SUBLANES) * SUBLANES
        v = s0[a:a + 1] + s1[:n_rows]
        if n_b < n_rows:
            v = jnp.where(b_ids < n_b, v, -jnp.inf)
        vals.append(v)
        ids.append(i0[a:a + 1] * N_KEYS + i1[:n_rows])
        a += 1
    vals.append(s0[a:] + s1[0:1])
    ids.append(i0[a:] * N_KEYS + i1[0:1])
    return jnp.concatenate(vals, axis=0), jnp.concatenate(ids, axis=0)


def _top_rows(s, payload, k):
    n = s.shape[0]
    rows = lax.broadcasted_iota(jnp.int32, s.shape, 0)
    vals, picks = [], []
    for _ in range(k):
        m = jnp.max(s, axis=0, keepdims=True)
        first = jnp.min(jnp.where(s == m, rows, n), axis=0, keepdims=True)
        hit = rows == first
        vals.append(m)
        if payload is None:
            picks.append(first)
        else:
            picks.append(jnp.max(jnp.where(hit, payload, -1), axis=0, keepdims=True))
        s = jnp.where(hit, -jnp.inf, s)
    return jnp.concatenate(vals, axis=0), jnp.concatenate(picks, axis=0)


def _peer_topk_kernel(xn_ref, wq_ref, keys_ref, off_ref, g_ref, q_scr, off_scr, g_scr):
    q_t = lax.dot_general(wq_ref[...], xn_ref[...].astype(BF16), (((1,), (1,)), ((), ())),
                          preferred_element_type=F32)
    q_scr[...] = q_t.astype(BF16)

    def head(h, carry):
        tops = []
        for half in range(2):
            hc = h * 2 + half
            q_hc = q_scr[pl.ds(pl.multiple_of(hc * D_HALF, D_HALF), D_HALF), :]
            s = jnp.dot(keys_ref[hc], q_hc, preferred_element_type=F32)
            tops.append(_top_rows(s, None, TOPK))
        (s0, i0), (s1, i1) = tops
        best, idx = _top_rows(*_pair_candidates(s0, i0, s1, i1), TOPK)
        e = jnp.exp(best - best[0:1])
        rows = pl.ds(pl.multiple_of(h * TOPK, TOPK), TOPK)
        off_scr[rows, :] = idx * WORD_ROWS
        g_scr[rows, :] = e / jnp.sum(e, axis=0, keepdims=True)
        return carry

    lax.fori_loop(0, PEER_HEADS, head, 0)
    off_ref[...] = off_scr[...].T
    g_ref[...] = g_scr[...].T


def _peer_topk(xn, w_query, sub_keys):
    t = xn.shape[0]
    nq = PEER_HEADS * D_QUERY
    out_blk = pl.BlockSpec((TOPK_TILE, N_PAIR), lambda i: (i, 0))
    return pl.pallas_call(
        _peer_topk_kernel,
        out_shape=[jax.ShapeDtypeStruct((t, N_PAIR), jnp.int32), jax.ShapeDtypeStruct((t, N_PAIR), F32)],
        grid=(t // TOPK_TILE,),
        in_specs=[pl.BlockSpec((TOPK_TILE, D_MODEL), lambda i: (i, 0)), _resident((nq, D_MODEL)),
                  _resident((2 * PEER_HEADS, N_KEYS, D_HALF))],
        out_specs=[out_blk, out_blk],
        scratch_shapes=[pltpu.VMEM((nq, TOPK_TILE), BF16), pltpu.VMEM((N_PAIR, TOPK_TILE), jnp.int32),
                        pltpu.VMEM((N_PAIR, TOPK_TILE), F32)],
        compiler_params=_params("parallel"),
        name="peer_topk",
    )(xn, w_query.T.astype(BF16), sub_keys.reshape(2 * PEER_HEADS, N_KEYS, D_HALF).astype(BF16))


def _pack_kernel(x_ref, o_ref):
    rows = x_ref.shape[0]
    for j in range(WORD_ROWS):
        lo = x_ref[:, j * LANES:(j + 1) * LANES]
        hi = x_ref[:, (WORD_ROWS + j) * LANES:(WORD_ROWS + j + 1) * LANES]
        o_ref[pl.ds(j, rows, stride=WORD_ROWS), :] = pltpu.pack_elementwise([lo, hi], packed_dtype=BF16)


def _pack_table(tab):
    n = tab.shape[0]
    return pl.pallas_call(
        _pack_kernel,
        out_shape=jax.ShapeDtypeStruct((n * WORD_ROWS, LANES), PACKED),
        grid=(n // ROW_TILE,),
        in_specs=[pl.BlockSpec((ROW_TILE, D_MODEL), lambda i: (i, 0))],
        out_specs=pl.BlockSpec((ROW_TILE * WORD_ROWS, LANES), lambda i: (i, 0)),
        compiler_params=_params("parallel"),
        name="pack_table",
    )(tab)


def _slot_chunk(slot):
    return slot // 2 + WORD_ROWS * (slot % 2)


def _chunk_sum_matrix():
    return (np.arange(D_MODEL)[:, None] // N_CHUNK == np.arange(N_PAIR)[None, :]).astype(np.float32)


def _gathered_tile(off_ref, tab_ref, t, j):
    parts = [tab_ref[pl.ds(pl.multiple_of(off_ref[t, TILE_PAIRS * j + i], WORD_ROWS), WORD_ROWS), :]
             for i in range(TILE_PAIRS)]
    return pltpu.bitcast(jnp.concatenate(parts, axis=0), BF16)


def _split_bf16(a):
    hi = a.astype(BF16)
    return jnp.concatenate([hi, (a - hi.astype(F32)).astype(BF16)], axis=0)


def _for_each_token(off_hbm, off_bufs, sem, compute):
    s = pl.program_id(0)

    def fetch(block, which):
        return pltpu.make_async_copy(off_hbm.at[block], off_bufs[which], sem.at[which])

    def consume(which):
        for t in range(PEER_HALF):
            compute(which * PEER_HALF + t, off_bufs[which], t)

    @pl.when(s == 0)
    def _():
        fetch(0, 0).start()

    fetch(2 * s, 0).wait()
    fetch(2 * s + 1, 1).start()
    consume(0)
    fetch(2 * s + 1, 1).wait()

    @pl.when(s + 1 < pl.num_programs(0))
    def _():
        fetch(2 * s + 2, 0).start()

    consume(1)


def _group_row0():
    return pl.multiple_of((pl.program_id(0) % STEPS_PER_GROUP) * PEER_STEP, PEER_STEP)


def _peer_u_kernel(off_hbm, x_ref, tab_ref, sum_ref, gate_ref, w_ref, off_a, off_b, sem, z_scr):
    shape = (2 * N_CHUNK, MXU_ROWS)
    own = (lax.broadcasted_iota(jnp.int32, shape, 0) % N_CHUNK) == _slot_chunk(lax.broadcasted_iota(jnp.int32, shape, 1) % N_CHUNK)
    row0 = _group_row0()

    def compute(tok, off_ref, t):
        lhs = _split_bf16(x_ref[tok])
        zs = []
        for j in range(N_TILES):
            o = lax.dot_general(lhs, _gathered_tile(off_ref, tab_ref, t, j), (((1,), (1,)), ((), ())),
                                preferred_element_type=F32)
            zs.append(jnp.sum(jnp.where(own, o, 0.0), axis=0, keepdims=True))
        z_scr[pl.ds(row0 + tok, 1), :] = jnp.concatenate(zs, axis=1)

    _for_each_token(off_hbm, (off_a, off_b), sem, compute)

    @pl.when(pl.program_id(0) % STEPS_PER_GROUP == STEPS_PER_GROUP - 1)
    def _():
        act = jnp.dot(z_scr[...], sum_ref[...], preferred_element_type=F32, precision=lax.Precision.HIGHEST)
        w_ref[...] = gate_ref[...] * _gelu(act)


def _peer_v_kernel(off_hbm, w_ref, x1_ref, tab_ref, spread_ref, y_ref, off_a, off_b, sem, wx_scr):
    shape = (N_CHUNK, D_MODEL)
    own = lax.broadcasted_iota(jnp.int32, shape, 0) == _slot_chunk(lax.broadcasted_iota(jnp.int32, shape, 1) % N_CHUNK)
    row0 = _group_row0()

    @pl.when(pl.program_id(0) % STEPS_PER_GROUP == 0)
    def _():
        wx_scr[...] = jnp.dot(w_ref[...], spread_ref[...], preferred_element_type=F32,
                              precision=lax.Precision.HIGHEST)

    def compute(tok, off_ref, t):
        lhs = _split_bf16(jnp.where(own, jnp.broadcast_to(wx_scr[pl.ds(row0 + tok, 1), :], shape), 0.0))
        r = None
        for j in range(N_TILES):
            part = jnp.dot(lhs[:, j * MXU_ROWS:(j + 1) * MXU_ROWS], _gathered_tile(off_ref, tab_ref, t, j),
                           preferred_element_type=F32)
            r = part if r is None else r + part
        y_ref[tok] = x1_ref[tok] + r[:N_CHUNK] + r[N_CHUNK:]

    _for_each_token(off_hbm, (off_a, off_b), sem, compute)


def _peer_specs():
    pairs = pl.BlockSpec((PEER_GROUP, N_PAIR), lambda i: (i // STEPS_PER_GROUP, 0))
    rows = pl.BlockSpec((PEER_STEP, N_CHUNK, LANES), lambda i: (i, 0, 0))
    table = _resident((N_EXPERTS * WORD_ROWS, LANES))
    scratch = ([pltpu.SMEM((PEER_HALF, N_PAIR), jnp.int32)] * 2 + [pltpu.SemaphoreType.DMA((2,))]
               + [pltpu.VMEM((PEER_GROUP, D_MODEL), F32)])
    return pl.BlockSpec(memory_space=pl.ANY), pairs, rows, table, scratch


def _peer_u(off, gate, xn, tab_u):
    t = off.shape[0]
    off_s, pairs, rows, table, scratch = _peer_specs()
    return pl.pallas_call(
        _peer_u_kernel,
        out_shape=jax.ShapeDtypeStruct((t, N_PAIR), F32),
        grid=(t // PEER_STEP,),
        in_specs=[off_s, rows, table, _resident((D_MODEL, N_PAIR)), pairs],
        out_specs=pairs,
        scratch_shapes=scratch,
        compiler_params=_params("arbitrary"),
        name="peer_u",
    )(off.reshape(t // PEER_HALF, PEER_HALF, N_PAIR), xn.reshape(t, N_CHUNK, LANES), tab_u,
      jnp.asarray(_chunk_sum_matrix()), gate)


def _peer_v(off, w, x1, tab_v):
    t = off.shape[0]
    off_s, pairs, rows, table, scratch = _peer_specs()
    y = pl.pallas_call(
        _peer_v_kernel,
        out_shape=jax.ShapeDtypeStruct((t, N_CHUNK, LANES), F32),
        grid=(t // PEER_STEP,),
        in_specs=[off_s, pairs, rows, table, _resident((N_PAIR, D_MODEL))],
        out_specs=rows,
        scratch_shapes=scratch,
        compiler_params=_params("arbitrary"),
        name="peer_v",
    )(off.reshape(t // PEER_HALF, PEER_HALF, N_PAIR), w, x1.reshape(t, N_CHUNK, LANES), tab_v,
      jnp.asarray(_chunk_sum_matrix().T))
    return y.reshape(t, D_MODEL)


def kernel(x, ln_mix_g, w_in, conv_w, conv_b, w_gate_a, b_gate_a, w_gate_x, b_gate_x, lru_L, q_norm_g,
           k_norm_g, sinks, lru_out_g, attn_out_g, w_out, ln_ffn_g, w_query, sub_keys, expert_u, expert_v,
           rel_bias):
    batch, seq, _ = x.shape
    bias = _band_bias(rel_bias)
    x2 = x.reshape(batch * seq, D_MODEL)
    for l in range(w_in.shape[0]):
        xb, gb, q, k, v = _inproj(x2, ln_mix_g[l], w_in[l])
        y_lru = _rglru(xb, gb, conv_w[l], conv_b[l], w_gate_a[l], b_gate_a[l], w_gate_x[l], b_gate_x[l],
                       lru_L[l], lru_out_g[l], batch, seq)
        y_att = _swa(q, k, v, bias, q_norm_g[l], k_norm_g[l], sinks[l], attn_out_g[l], batch, seq)
        x1, xn = _outproj(x2, y_lru, y_att, w_out[l], ln_ffn_g[l])
        off, gate = _peer_topk(xn, w_query[l], sub_keys[l])
        w = _peer_u(off, gate, xn, _pack_table(expert_u[l]))
        x2 = _peer_v(off, w, x1, _pack_table(expert_v[l]))
    return x2.reshape(batch, seq, D_MODEL)
```

```python
import math

import jax
import jax.numpy as jnp
import numpy as np
from jax import lax
from jax.experimental import pallas as pl
from jax.experimental.pallas import tpu as pltpu

D_MODEL = 1024
LRU_WIDTH = 512
LRU_BLOCKS = 8
LRU_BLOCK = LRU_WIDTH // LRU_BLOCKS
CONV_WIDTH = 4
LRU_C = 8.0
N_HEADS = 8
N_KV_HEADS = 2
GROUP = N_HEADS // N_KV_HEADS
HEAD_DIM = 64
ATTN_WIDTH = N_HEADS * HEAD_DIM
KV_WIDTH = N_KV_HEADS * HEAD_DIM
WINDOW = 128
BLOCK = 128
N_BUCKETS = 32
MAX_DISTANCE = 128
PEER_HEADS = 8
N_KEYS = 128
N_EXPERTS = N_KEYS * N_KEYS
D_QUERY = 256
D_HALF = D_QUERY // 2
TOPK = 16
N_PAIR = PEER_HEADS * TOPK
IN_COLS = 2 * LRU_WIDTH + ATTN_WIDTH + 2 * KV_WIDTH
EPS = 1e-6
NEG_INF = -1e30
SCALE = HEAD_DIM ** -0.5

F32 = jnp.float32
BF16 = jnp.bfloat16
PACKED = jnp.uint32
LANES = 128
SUBLANES = 8
N_CHUNK = D_MODEL // LANES
WORD_ROWS = N_CHUNK // 2
MXU_ROWS = 256
TILE_PAIRS = MXU_ROWS // N_CHUNK
N_TILES = N_PAIR // TILE_PAIRS
VMEM_LIMIT = 52 * 1024 * 1024

ROW_TILE = 512
LRU_TILE = 256
SWA_BLOCKS = 2
TOPK_TILE = 512
PEER_HALF = 64
PEER_STEP = 2 * PEER_HALF
PEER_GROUP = 256
STEPS_PER_GROUP = PEER_GROUP // PEER_STEP


def _rms(x, g):
    return x * lax.rsqrt(jnp.mean(x * x, axis=-1, keepdims=True) + EPS) * g


def _gelu(x):
    return 0.5 * x * (1.0 + jnp.tanh(math.sqrt(2.0 / math.pi) * (x + 0.044715 * (x * x * x))))


def _params(*sem):
    return pltpu.CompilerParams(dimension_semantics=sem, vmem_limit_bytes=VMEM_LIMIT)


def _resident(shape):
    zeros = (0,) * len(shape)
    return pl.BlockSpec(shape, lambda *_: zeros, pipeline_mode=pl.Buffered(1))


def _inproj_kernel(x_ref, g_ref, w_ref, xb_ref, gb_ref, q_ref, k_ref, v_ref):
    h = _rms(x_ref[...], g_ref[...])
    p = jnp.dot(h.astype(BF16), w_ref[...], preferred_element_type=F32)
    c0, c1, c2, c3 = LRU_WIDTH, 2 * LRU_WIDTH, 2 * LRU_WIDTH + ATTN_WIDTH, IN_COLS - KV_WIDTH
    xb_ref[...] = p[:, :c0]
    gb_ref[...] = p[:, c0:c1]
    q_ref[...] = p[:, c1:c2]
    k_ref[...] = p[:, c2:c3]
    v_ref[...] = p[:, c3:]


def _inproj(x2, g, w_in):
    t = x2.shape[0]
    widths = (LRU_WIDTH, LRU_WIDTH, ATTN_WIDTH, KV_WIDTH, KV_WIDTH)
    row = lambda w: pl.BlockSpec((ROW_TILE, w), lambda i: (i, 0))
    return pl.pallas_call(
        _inproj_kernel,
        out_shape=[jax.ShapeDtypeStruct((t, w), F32) for w in widths],
        grid=(t // ROW_TILE,),
        in_specs=[row(D_MODEL), _resident((1, D_MODEL)), _resident((D_MODEL, IN_COLS))],
        out_specs=[row(w) for w in widths],
        compiler_params=_params("parallel"),
        name="inproj",
    )(x2, g.reshape(1, D_MODEL), w_in.astype(BF16))


def _rglru_kernel(xb_ref, gb_ref, cw_ref, cb_ref, wg_ref, bg_ref, l_ref, og_ref, o_ref,
                  xs_scr, a_scr, b_scr, h_scr):
    tt, c = xb_ref.shape

    @pl.when(pl.program_id(1) == 0)
    def _():
        xs_scr[0:SUBLANES, :] = jnp.zeros((SUBLANES, c), F32)
        h_scr[...] = jnp.zeros_like(h_scr)

    xb = xb_ref[...]
    xs_scr[SUBLANES:SUBLANES + tt, :] = xb
    xc = cb_ref[...] + xb * cw_ref[CONV_WIDTH - 1:CONV_WIDTH, :]
    for back in range(1, CONV_WIDTH):
        tap = CONV_WIDTH - 1 - back
        xc = xc + xs_scr[SUBLANES - back:SUBLANES - back + tt, :] * cw_ref[tap:tap + 1, :]
    xs_scr[0:SUBLANES, :] = xb[tt - SUBLANES:, :]

    gates = jnp.dot(xc.astype(BF16), wg_ref[...], preferred_element_type=F32) + bg_ref[...]
    r = jax.nn.sigmoid(gates[:, :c])
    ig = jax.nn.sigmoid(gates[:, c:])
    lam = l_ref[...]
    softplus_neg = jnp.maximum(-lam, 0.0) + jnp.log1p(jnp.exp(-jnp.abs(lam)))
    log_a = (-LRU_C) * r * softplus_neg
    a = jnp.exp(log_a)
    b = jnp.sqrt(-jnp.tanh(log_a) * (a * a + 1.0)) * (ig * xc)
    a_scr[...] = a
    b_scr[...] = b

    rows = lax.broadcasted_iota(jnp.int32, (SUBLANES, c), 0)

    def tile(i, h):
        off = pl.multiple_of(i * SUBLANES, SUBLANES)
        at = a_scr[pl.ds(off, SUBLANES), :]
        bt = b_scr[pl.ds(off, SUBLANES), :]
        for d in (1, 2, 4):
            keep = rows >= d
            bt = jnp.where(keep, at * pltpu.roll(bt, d, axis=0) + bt, bt)
            at = jnp.where(keep, at * pltpu.roll(at, d, axis=0), at)
        ht = at * h + bt
        b_scr[pl.ds(off, SUBLANES), :] = ht
        return ht[SUBLANES - 1:SUBLANES, :]

    h_scr[...] = lax.fori_loop(0, tt // SUBLANES, tile, h_scr[...])
    y = b_scr[...] * _gelu(gb_ref[...])
    o_ref[...] = _rms(y, og_ref[...]).astype(o_ref.dtype)


def _rglru(xb, gb, conv_w, conv_b, w_gate_a, b_gate_a, w_gate_x, b_gate_x, lru_l, out_g, batch, seq):
    c = LRU_WIDTH
    eye = jnp.eye(LRU_BLOCKS, dtype=F32)
    dense = lambda w: jnp.einsum("nij,nm->nimj", w, eye).reshape(c, c)
    wg = jnp.concatenate([dense(w_gate_a), dense(w_gate_x)], axis=1).astype(BF16)
    bg = jnp.concatenate([b_gate_a.reshape(1, c), b_gate_x.reshape(1, c)], axis=1)
    blk = pl.BlockSpec((None, LRU_TILE, c), lambda b, j: (b, j, 0))
    out = pl.pallas_call(
        _rglru_kernel,
        out_shape=jax.ShapeDtypeStruct((batch, seq, c), BF16),
        grid=(batch, seq // LRU_TILE),
        in_specs=[blk, blk, _resident((CONV_WIDTH, c)), _resident((1, c)), _resident((c, 2 * c)),
                  _resident((1, 2 * c)), _resident((1, c)), _resident((1, c))],
        out_specs=blk,
        scratch_shapes=[pltpu.VMEM((LRU_TILE + SUBLANES, c), F32), pltpu.VMEM((LRU_TILE, c), F32),
                        pltpu.VMEM((LRU_TILE, c), F32), pltpu.VMEM((1, c), F32)],
        compiler_params=_params("parallel", "arbitrary"),
        name="rglru",
    )(xb.reshape(batch, seq, c), gb.reshape(batch, seq, c), conv_w, conv_b.reshape(1, c), wg, bg,
      lru_l.reshape(1, c), out_g.reshape(1, c))
    return out.reshape(batch * seq, c)


def _bias_kernel(rb_ref, onehot_ref, o_ref):
    o_ref[...] = jnp.dot(rb_ref[...], onehot_ref[...], preferred_element_type=F32,
                         precision=lax.Precision.HIGHEST)


def _t5_bucket(rel):
    n = jnp.maximum(rel, 0)
    max_exact = N_BUCKETS // 2
    nf = jnp.maximum(n, 1).astype(F32)
    large = max_exact + jnp.floor(jnp.log(nf / max_exact) / math.log(MAX_DISTANCE / max_exact)
                                  * (N_BUCKETS - max_exact)).astype(jnp.int32)
    large = jnp.minimum(large, N_BUCKETS - 1)
    return jnp.where(n < max_exact, n, large)


def _band_bias(rel_bias):
    i = jnp.arange(BLOCK)[:, None]
    j = jnp.arange(2 * BLOCK)[None, :]
    bucket = _t5_bucket(BLOCK + i - j).reshape(1, -1)
    onehot = (bucket == jnp.arange(N_BUCKETS)[:, None]).astype(F32)
    out = pl.pallas_call(
        _bias_kernel,
        out_shape=jax.ShapeDtypeStruct((N_HEADS, BLOCK * 2 * BLOCK), F32),
        name="band_bias",
    )(rel_bias.astype(F32).T, onehot)
    return out.reshape(N_HEADS, BLOCK, 2 * BLOCK)


def _swa_kernel(sink_ref, q_ref, kc_ref, kp_ref, vc_ref, vp_ref, bias_ref, qg_ref, kg_ref, og_ref, o_ref):
    kk = jnp.concatenate([kp_ref[...], kc_ref[...]], axis=0)
    vv = jnp.concatenate([vp_ref[...], vc_ref[...]], axis=0).astype(BF16)
    k_n = [_rms(kk[:, hk * HEAD_DIM:(hk + 1) * HEAD_DIM], kg_ref[...]).astype(BF16) for hk in range(N_KV_HEADS)]
    shape = (GROUP * BLOCK, 2 * BLOCK)
    qi = lax.broadcasted_iota(jnp.int32, shape, 0) % BLOCK
    kj = lax.broadcasted_iota(jnp.int32, shape, 1)
    for sub in range(SWA_BLOCKS):
        first_key = jnp.where(pl.program_id(1) > 0, 0, BLOCK) if sub == 0 else 0
        valid = (kj > jnp.maximum(qi + (BLOCK - WINDOW), first_key - 1)) & (kj <= qi + BLOCK)
        rows = slice(sub * BLOCK, (sub + 1) * BLOCK)
        keys = slice(sub * BLOCK, (sub + 2) * BLOCK)
        outs = []
        for hk in range(N_KV_HEADS):
            qs, sinks = [], []
            for g in range(GROUP):
                h = hk * GROUP + g
                qs.append(_rms(q_ref[rows, h * HEAD_DIM:(h + 1) * HEAD_DIM], qg_ref[...]).astype(BF16))
                sinks.append(jnp.full((BLOCK, 1), sink_ref[h], F32))
            q_n = jnp.concatenate(qs, axis=0)
            sink = jnp.concatenate(sinks, axis=0)
            s = lax.dot_general(q_n, k_n[hk][keys], (((1,), (1,)), ((), ())), preferred_element_type=F32)
            s = s * SCALE + bias_ref[hk * GROUP:(hk + 1) * GROUP].reshape(shape)
            s = jnp.where(valid, s, NEG_INF)
            m = jnp.maximum(jnp.max(s, axis=-1, keepdims=True), sink)
            p = jnp.exp(s - m)
            denom = jnp.sum(p, axis=-1, keepdims=True) + jnp.exp(sink - m)
            o = jnp.dot(p.astype(BF16), vv[keys, hk * HEAD_DIM:(hk + 1) * HEAD_DIM],
                        preferred_element_type=F32) / denom
            outs.extend(o[g * BLOCK:(g + 1) * BLOCK] for g in range(GROUP))
        y = jnp.concatenate(outs, axis=-1)
        o_ref[rows, :] = _rms(y, og_ref[...]).astype(o_ref.dtype)


def _swa(q, k, v, bias, q_norm_g, k_norm_g, sinks, out_g, batch, seq):
    nb = seq // BLOCK
    cur = lambda w: pl.BlockSpec((None, SWA_BLOCKS * BLOCK, w), lambda b, n: (b, n, 0))
    prev = lambda w: pl.BlockSpec((None, BLOCK, w), lambda b, n: (b, jnp.maximum(SWA_BLOCKS * n - 1, 0), 0))
    k3 = k.reshape(batch, seq, KV_WIDTH)
    v3 = v.reshape(batch, seq, KV_WIDTH)
    out = pl.pallas_call(
        _swa_kernel,
        out_shape=jax.ShapeDtypeStruct((batch, seq, ATTN_WIDTH), BF16),
        grid=(batch, nb // SWA_BLOCKS),
        in_specs=[pl.BlockSpec(memory_space=pltpu.SMEM), cur(ATTN_WIDTH), cur(KV_WIDTH), prev(KV_WIDTH),
                  cur(KV_WIDTH), prev(KV_WIDTH), _resident((N_HEADS, BLOCK, 2 * BLOCK)),
                  _resident((1, HEAD_DIM)), _resident((1, HEAD_DIM)), _resident((1, ATTN_WIDTH))],
        out_specs=cur(ATTN_WIDTH),
        compiler_params=_params("parallel", "parallel"),
        name="swa",
    )(sinks.astype(F32), q.reshape(batch, seq, ATTN_WIDTH), k3, k3, v3, v3, bias,
      q_norm_g.reshape(1, HEAD_DIM), k_norm_g.reshape(1, HEAD_DIM), out_g.reshape(1, ATTN_WIDTH))
    return out.reshape(batch * seq, ATTN_WIDTH)


def _outproj_kernel(x_ref, ml_ref, ma_ref, w_ref, g_ref, x1_ref, xn_ref):
    acc = jnp.dot(ml_ref[...], w_ref[:LRU_WIDTH, :], preferred_element_type=F32)
    acc = acc + jnp.dot(ma_ref[...], w_ref[LRU_WIDTH:, :], preferred_element_type=F32)
    x1 = x_ref[...] + acc
    x1_ref[...] = x1
    xn_ref[...] = _rms(x1, g_ref[...])


def _outproj(x2, y_lru, y_att, w_out, g):
    t = x2.shape[0]
    row = lambda w: pl.BlockSpec((ROW_TILE, w), lambda i: (i, 0))
    return pl.pallas_call(
        _outproj_kernel,
        out_shape=[jax.ShapeDtypeStruct((t, D_MODEL), F32)] * 2,
        grid=(t // ROW_TILE,),
        in_specs=[row(D_MODEL), row(LRU_WIDTH), row(ATTN_WIDTH), _resident((D_MODEL, D_MODEL)),
                  _resident((1, D_MODEL))],
        out_specs=[row(D_MODEL)] * 2,
        compiler_params=_params("parallel"),
        name="outproj",
    )(x2, y_lru, y_att, w_out.astype(BF16), g.reshape(1, D_MODEL))


def _pair_candidates(s0, i0, s1, i1):
    b_ids = lax.broadcasted_iota(jnp.int32, (SUBLANES,) + s1.shape[1:], 0)
    vals, ids = [], []
    a = 0
    while TOPK // (a + 1) > 1:
        n_b = TOPK // (a + 1)
        n_rows = -(-n_b // SUBLANES) * SUBLANES
        v = s0[a:a + 1] + s1[:n_rows]
        if n_b < n_rows:
            v = jnp.where(b_ids < n_b, v, -jnp.inf)
        vals.append(v)
        ids.append(i0[a:a + 1] * N_KEYS + i1[:n_rows])
        a += 1
    vals.append(s0[a:] + s1[0:1])
    ids.append(i0[a:] * N_KEYS + i1[0:1])
    return jnp.concatenate(vals, axis=0), jnp.concatenate(ids, axis=0)


def _top_rows(s, payload, k):
    n = s.shape[0]
    rows = lax.broadcasted_iota(jnp.int32, s.shape, 0).astype(F32)
    vals, picks = [], []
    for _ in range(k):
        m = jnp.max(s, axis=0, keepdims=True)
        first = jnp.min(jnp.where(s == m, rows, float(n)), axis=0, keepdims=True)
        hit = rows == first
        vals.append(m)
        if payload is None:
            picks.append(first)
        else:
            picks.append(jnp.max(jnp.where(hit, payload, -1.0), axis=0, keepdims=True))
        s = jnp.where(hit, -jnp.inf, s)
    return jnp.concatenate(vals, axis=0), jnp.concatenate(picks, axis=0)


def _peer_topk_kernel(xn_ref, wq_ref, keys_ref, off_ref, g_ref, q_scr, off_scr, g_scr):
    q_t = lax.dot_general(wq_ref[...], xn_ref[...].astype(BF16), (((1,), (1,)), ((), ())),
                          preferred_element_type=F32)
    q_scr[...] = q_t.astype(BF16)

    def head(h, carry):
        tops = []
        for half in range(2):
            hc = h * 2 + half
            q_hc = q_scr[pl.ds(pl.multiple_of(hc * D_HALF, D_HALF), D_HALF), :]
            s = jnp.dot(keys_ref[hc], q_hc, preferred_element_type=F32)
            tops.append(_top_rows(s, None, TOPK))
        (s0, i0), (s1, i1) = tops
        best, idx = _top_rows(*_pair_candidates(s0, i0, s1, i1), TOPK)
        e = jnp.exp(best - best[0:1])
        rows = pl.ds(pl.multiple_of(h * TOPK, TOPK), TOPK)
        off_scr[rows, :] = (idx * WORD_ROWS).astype(jnp.int32)
        g_scr[rows, :] = e / jnp.sum(e, axis=0, keepdims=True)
        return carry

    lax.fori_loop(0, PEER_HEADS, head, 0)
    off_ref[...] = off_scr[...].T
    g_ref[...] = g_scr[...].T


def _peer_topk(xn, w_query, sub_keys):
    t = xn.shape[0]
    nq = PEER_HEADS * D_QUERY
    out_blk = pl.BlockSpec((TOPK_TILE, N_PAIR), lambda i: (i, 0))
    return pl.pallas_call(
        _peer_topk_kernel,
        out_shape=[jax.ShapeDtypeStruct((t, N_PAIR), jnp.int32), jax.ShapeDtypeStruct((t, N_PAIR), F32)],
        grid=(t // TOPK_TILE,),
        in_specs=[pl.BlockSpec((TOPK_TILE, D_MODEL), lambda i: (i, 0)), _resident((nq, D_MODEL)),
                  _resident((2 * PEER_HEADS, N_KEYS, D_HALF))],
        out_specs=[out_blk, out_blk],
        scratch_shapes=[pltpu.VMEM((nq, TOPK_TILE), BF16), pltpu.VMEM((N_PAIR, TOPK_TILE), jnp.int32),
                        pltpu.VMEM((N_PAIR, TOPK_TILE), F32)],
        compiler_params=_params("parallel"),
        name="peer_topk",
    )(xn, w_query.T.astype(BF16), sub_keys.reshape(2 * PEER_HEADS, N_KEYS, D_HALF).astype(BF16))


def _pack_kernel(x_ref, o_ref):
    rows = x_ref.shape[0]
    for j in range(WORD_ROWS):
        lo = x_ref[:, j * LANES:(j + 1) * LANES]
        hi = x_ref[:, (WORD_ROWS + j) * LANES:(WORD_ROWS + j + 1) * LANES]
        o_ref[pl.ds(j, rows, stride=WORD_ROWS), :] = pltpu.pack_elementwise([lo, hi], packed_dtype=BF16)


def _pack_table(tab):
    n = tab.shape[0]
    return pl.pallas_call(
        _pack_kernel,
        out_shape=jax.ShapeDtypeStruct((n * WORD_ROWS, LANES), PACKED),
        grid=(n // ROW_TILE,),
        in_specs=[pl.BlockSpec((ROW_TILE, D_MODEL), lambda i: (i, 0))],
        out_specs=pl.BlockSpec((ROW_TILE * WORD_ROWS, LANES), lambda i: (i, 0)),
        compiler_params=_params("parallel"),
        name="pack_table",
    )(tab)


def _slot_chunk(slot):
    return slot // 2 + WORD_ROWS * (slot % 2)


def _chunk_sum_matrix():
    return (np.arange(D_MODEL)[:, None] // N_CHUNK == np.arange(N_PAIR)[None, :]).astype(np.float32)


def _gathered_tile(off_ref, tab_ref, t, j):
    parts = [tab_ref[pl.ds(pl.multiple_of(off_ref[t, TILE_PAIRS * j + i], WORD_ROWS), WORD_ROWS), :]
             for i in range(TILE_PAIRS)]
    return pltpu.bitcast(jnp.concatenate(parts, axis=0), BF16)


def _select_dot(a, sel):
    out = None
    for _ in range(3):
        piece = a.astype(BF16)
        part = jnp.dot(piece, sel, preferred_element_type=F32)
        out = part if out is None else out + part
        a = a - piece.astype(F32)
    return out


def _split_bf16(a):
    hi = a.astype(BF16)
    return jnp.concatenate([hi, (a - hi.astype(F32)).astype(BF16)], axis=0)


def _for_each_token(off_hbm, off_bufs, sem, compute):
    s = pl.program_id(0)

    def fetch(block, which):
        return pltpu.make_async_copy(off_hbm.at[block], off_bufs[which], sem.at[which])

    def consume(which):
        for t in range(PEER_HALF):
            compute(which * PEER_HALF + t, off_bufs[which], t)

    @pl.when(s == 0)
    def _():
        fetch(0, 0).start()

    fetch(2 * s, 0).wait()
    fetch(2 * s + 1, 1).start()
    consume(0)
    fetch(2 * s + 1, 1).wait()

    @pl.when(s + 1 < pl.num_programs(0))
    def _():
        fetch(2 * s + 2, 0).start()

    consume(1)


def _group_row0():
    return pl.multiple_of((pl.program_id(0) % STEPS_PER_GROUP) * PEER_STEP, PEER_STEP)


def _peer_u_kernel(off_hbm, x_ref, tab_ref, sum_ref, gate_ref, w_ref, off_a, off_b, sem, z_scr):
    shape = (2 * N_CHUNK, MXU_ROWS)
    own = (lax.broadcasted_iota(jnp.int32, shape, 0) % N_CHUNK) == _slot_chunk(lax.broadcasted_iota(jnp.int32, shape, 1) % N_CHUNK)
    row0 = _group_row0()

    def compute(tok, off_ref, t):
        lhs = _split_bf16(x_ref[tok])
        zs = []
        for j in range(N_TILES):
            o = lax.dot_general(lhs, _gathered_tile(off_ref, tab_ref, t, j), (((1,), (1,)), ((), ())),
                                preferred_element_type=F32)
            zs.append(jnp.sum(jnp.where(own, o, 0.0), axis=0, keepdims=True))
        z_scr[pl.ds(row0 + tok, 1), :] = jnp.concatenate(zs, axis=1)

    _for_each_token(off_hbm, (off_a, off_b), sem, compute)

    @pl.when(pl.program_id(0) % STEPS_PER_GROUP == STEPS_PER_GROUP - 1)
    def _():
        act = _select_dot(z_scr[...], sum_ref[...])
        w_ref[...] = gate_ref[...] * _gelu(act)


def _peer_v_kernel(off_hbm, w_ref, x1_ref, tab_ref, spread_ref, y_ref, off_a, off_b, sem, wx_scr):
    shape = (N_CHUNK, D_MODEL)
    own = lax.broadcasted_iota(jnp.int32, shape, 0) == _slot_chunk(lax.broadcasted_iota(jnp.int32, shape, 1) % N_CHUNK)
    row0 = _group_row0()

    @pl.when(pl.program_id(0) % STEPS_PER_GROUP == 0)
    def _():
        wx_scr[...] = _select_dot(w_ref[...], spread_ref[...])

    def compute(tok, off_ref, t):
        lhs = _split_bf16(jnp.where(own, jnp.broadcast_to(wx_scr[pl.ds(row0 + tok, 1), :], shape), 0.0))
        r = None
        for j in range(N_TILES):
            part = jnp.dot(lhs[:, j * MXU_ROWS:(j + 1) * MXU_ROWS], _gathered_tile(off_ref, tab_ref, t, j),
                           preferred_element_type=F32)
            r = part if r is None else r + part
        y_ref[tok] = x1_ref[tok] + r[:N_CHUNK] + r[N_CHUNK:]

    _for_each_token(off_hbm, (off_a, off_b), sem, compute)


def _peer_specs():
    pairs = pl.BlockSpec((PEER_GROUP, N_PAIR), lambda i: (i // STEPS_PER_GROUP, 0))
    rows = pl.BlockSpec((PEER_STEP, N_CHUNK, LANES), lambda i: (i, 0, 0))
    table = _resident((N_EXPERTS * WORD_ROWS, LANES))
    scratch = ([pltpu.SMEM((PEER_HALF, N_PAIR), jnp.int32)] * 2 + [pltpu.SemaphoreType.DMA((2,))]
               + [pltpu.VMEM((PEER_GROUP, D_MODEL), F32)])
    return pl.BlockSpec(memory_space=pl.ANY), pairs, rows, table, scratch


def _peer_u(off, gate, xn, tab_u):
    t = off.shape[0]
    off_s, pairs, rows, table, scratch = _peer_specs()
    return pl.pallas_call(
        _peer_u_kernel,
        out_shape=jax.ShapeDtypeStruct((t, N_PAIR), F32),
        grid=(t // PEER_STEP,),
        in_specs=[off_s, rows, table, _resident((D_MODEL, N_PAIR)), pairs],
        out_specs=pairs,
        scratch_shapes=scratch,
        compiler_params=_params("arbitrary"),
        name="peer_u",
    )(off.reshape(t // PEER_HALF, PEER_HALF, N_PAIR), xn.reshape(t, N_CHUNK, LANES), tab_u,
      jnp.asarray(_chunk_sum_matrix(), BF16), gate)


def _peer_v(off, w, x1, tab_v):
    t = off.shape[0]
    off_s, pairs, rows, table, scratch = _peer_specs()
    y = pl.pallas_call(
        _peer_v_kernel,
        out_shape=jax.ShapeDtypeStruct((t, N_CHUNK, LANES), F32),
        grid=(t // PEER_STEP,),
        in_specs=[off_s, pairs, rows, table, _resident((N_PAIR, D_MODEL))],
        out_specs=rows,
        scratch_shapes=scratch,
        compiler_params=_params("arbitrary"),
        name="peer_v",
    )(off.reshape(t // PEER_HALF, PEER_HALF, N_PAIR), w, x1.reshape(t, N_CHUNK, LANES), tab_v,
      jnp.asarray(_chunk_sum_matrix().T, BF16))
    return y.reshape(t, D_MODEL)


def kernel(x, ln_mix_g, w_in, conv_w, conv_b, w_gate_a, b_gate_a, w_gate_x, b_gate_x, lru_L, q_norm_g,
           k_norm_g, sinks, lru_out_g, attn_out_g, w_out, ln_ffn_g, w_query, sub_keys, expert_u, expert_v,
           rel_bias):
    batch, seq, _ = x.shape
    bias = _band_bias(rel_bias)
    x2 = x.reshape(batch * seq, D_MODEL)
    for l in range(w_in.shape[0]):
        xb, gb, q, k, v = _inproj(x2, ln_mix_g[l], w_in[l])
        y_lru = _rglru(xb, gb, conv_w[l], conv_b[l], w_gate_a[l], b_gate_a[l], w_gate_x[l], b_gate_x[l],
                       lru_L[l], lru_out_g[l], batch, seq)
        y_att = _swa(q, k, v, bias, q_norm_g[l], k_norm_g[l], sinks[l], attn_out_g[l], batch, seq)
        x1, xn = _outproj(x2, y_lru, y_att, w_out[l], ln_ffn_g[l])
        off, gate = _peer_topk(xn, w_query[l], sub_keys[l])
        w = _peer_u(off, gate, xn, _pack_table(expert_u[l]))
        x2 = _peer_v(off, w, x1, _pack_table(expert_v[l]))
    return x2.reshape(batch, seq, D_MODEL)
```

```python
import math

import jax
import jax.numpy as jnp
import numpy as np
from jax import lax
from jax.experimental import pallas as pl
from jax.experimental.pallas import tpu as pltpu

D_MODEL = 1024
LRU_WIDTH = 512
LRU_BLOCKS = 8
LRU_BLOCK = LRU_WIDTH // LRU_BLOCKS
CONV_WIDTH = 4
LRU_C = 8.0
N_HEADS = 8
N_KV_HEADS = 2
GROUP = N_HEADS // N_KV_HEADS
HEAD_DIM = 64
ATTN_WIDTH = N_HEADS * HEAD_DIM
KV_WIDTH = N_KV_HEADS * HEAD_DIM
WINDOW = 128
BLOCK = 128
N_BUCKETS = 32
MAX_DISTANCE = 128
PEER_HEADS = 8
N_KEYS = 128
N_EXPERTS = N_KEYS * N_KEYS
D_QUERY = 256
D_HALF = D_QUERY // 2
TOPK = 16
N_PAIR = PEER_HEADS * TOPK
IN_COLS = 2 * LRU_WIDTH + ATTN_WIDTH + 2 * KV_WIDTH
EPS = 1e-6
NEG_INF = -1e30
SCALE = HEAD_DIM ** -0.5

F32 = jnp.float32
BF16 = jnp.bfloat16
PACKED = jnp.uint32
LANES = 128
SUBLANES = 8
N_CHUNK = D_MODEL // LANES
WORD_ROWS = N_CHUNK // 2
MXU_ROWS = 256
TILE_PAIRS = MXU_ROWS // N_CHUNK
N_TILES = N_PAIR // TILE_PAIRS
VMEM_LIMIT = 52 * 1024 * 1024

ROW_TILE = 512
LRU_TILE = 256
SWA_BLOCKS = 2
TOPK_TILE = 512
PEER_HALF = 64
PEER_STEP = 2 * PEER_HALF
PEER_GROUP = 256
STEPS_PER_GROUP = PEER_GROUP // PEER_STEP


def _rms(x, g):
    return x * lax.rsqrt(jnp.mean(x * x, axis=-1, keepdims=True) + EPS) * g


def _gelu(x):
    return 0.5 * x * (1.0 + jnp.tanh(math.sqrt(2.0 / math.pi) * (x + 0.044715 * (x * x * x))))


def _params(*sem):
    return pltpu.CompilerParams(dimension_semantics=sem, vmem_limit_bytes=VMEM_LIMIT)


def _resident(shape):
    zeros = (0,) * len(shape)
    return pl.BlockSpec(shape, lambda *_: zeros, pipeline_mode=pl.Buffered(1))


def _inproj_kernel(x_ref, g_ref, w_ref, xb_ref, gb_ref, q_ref, k_ref, v_ref):
    h = _rms(x_ref[...], g_ref[...])
    p = jnp.dot(h.astype(BF16), w_ref[...], preferred_element_type=F32)
    c0, c1, c2, c3 = LRU_WIDTH, 2 * LRU_WIDTH, 2 * LRU_WIDTH + ATTN_WIDTH, IN_COLS - KV_WIDTH
    xb_ref[...] = p[:, :c0]
    gb_ref[...] = p[:, c0:c1]
    q_ref[...] = p[:, c1:c2]
    k_ref[...] = p[:, c2:c3]
    v_ref[...] = p[:, c3:]


def _inproj(x2, g, w_in):
    t = x2.shape[0]
    widths = (LRU_WIDTH, LRU_WIDTH, ATTN_WIDTH, KV_WIDTH, KV_WIDTH)
    row = lambda w: pl.BlockSpec((ROW_TILE, w), lambda i: (i, 0))
    return pl.pallas_call(
        _inproj_kernel,
        out_shape=[jax.ShapeDtypeStruct((t, w), F32) for w in widths],
        grid=(t // ROW_TILE,),
        in_specs=[row(D_MODEL), _resident((1, D_MODEL)), _resident((D_MODEL, IN_COLS))],
        out_specs=[row(w) for w in widths],
        compiler_params=_params("parallel"),
        name="inproj",
    )(x2, g.reshape(1, D_MODEL), w_in.astype(BF16))


def _rglru_kernel(xb_ref, gb_ref, cw_ref, cb_ref, wg_ref, bg_ref, l_ref, og_ref, o_ref,
                  xs_scr, a_scr, b_scr, h_scr):
    tt, c = xb_ref.shape

    @pl.when(pl.program_id(1) == 0)
    def _():
        xs_scr[0:SUBLANES, :] = jnp.zeros((SUBLANES, c), F32)
        h_scr[...] = jnp.zeros_like(h_scr)

    xb = xb_ref[...]
    xs_scr[SUBLANES:SUBLANES + tt, :] = xb
    xc = cb_ref[...] + xb * cw_ref[CONV_WIDTH - 1:CONV_WIDTH, :]
    for back in range(1, CONV_WIDTH):
        tap = CONV_WIDTH - 1 - back
        xc = xc + xs_scr[SUBLANES - back:SUBLANES - back + tt, :] * cw_ref[tap:tap + 1, :]
    xs_scr[0:SUBLANES, :] = xb[tt - SUBLANES:, :]

    gates = jnp.dot(xc.astype(BF16), wg_ref[...], preferred_element_type=F32) + bg_ref[...]
    r = jax.nn.sigmoid(gates[:, :c])
    ig = jax.nn.sigmoid(gates[:, c:])
    lam = l_ref[...]
    softplus_neg = jnp.maximum(-lam, 0.0) + jnp.log1p(jnp.exp(-jnp.abs(lam)))
    log_a = (-LRU_C) * r * softplus_neg
    a = jnp.exp(log_a)
    b = jnp.sqrt(-jnp.tanh(log_a) * (a * a + 1.0)) * (ig * xc)
    a_scr[...] = a
    b_scr[...] = b

    rows = lax.broadcasted_iota(jnp.int32, (SUBLANES, c), 0)

    def tile(i, h):
        off = pl.multiple_of(i * SUBLANES, SUBLANES)
        at = a_scr[pl.ds(off, SUBLANES), :]
        bt = b_scr[pl.ds(off, SUBLANES), :]
        for d in (1, 2, 4):
            keep = rows >= d
            bt = jnp.where(keep, at * pltpu.roll(bt, d, axis=0) + bt, bt)
            at = jnp.where(keep, at * pltpu.roll(at, d, axis=0), at)
        ht = at * h + bt
        b_scr[pl.ds(off, SUBLANES), :] = ht
        return ht[SUBLANES - 1:SUBLANES, :]

    h_scr[...] = lax.fori_loop(0, tt // SUBLANES, tile, h_scr[...])
    y = b_scr[...] * _gelu(gb_ref[...])
    o_ref[...] = _rms(y, og_ref[...]).astype(o_ref.dtype)


def _rglru(xb, gb, conv_w, conv_b, w_gate_a, b_gate_a, w_gate_x, b_gate_x, lru_l, out_g, batch, seq):
    c = LRU_WIDTH
    eye = jnp.eye(LRU_BLOCKS, dtype=F32)
    dense = lambda w: jnp.einsum("nij,nm->nimj", w, eye).reshape(c, c)
    wg = jnp.concatenate([dense(w_gate_a), dense(w_gate_x)], axis=1).astype(BF16)
    bg = jnp.concatenate([b_gate_a.reshape(1, c), b_gate_x.reshape(1, c)], axis=1)
    blk = pl.BlockSpec((None, LRU_TILE, c), lambda b, j: (b, j, 0))
    out = pl.pallas_call(
        _rglru_kernel,
        out_shape=jax.ShapeDtypeStruct((batch, seq, c), BF16),
        grid=(batch, seq // LRU_TILE),
        in_specs=[blk, blk, _resident((CONV_WIDTH, c)), _resident((1, c)), _resident((c, 2 * c)),
                  _resident((1, 2 * c)), _resident((1, c)), _resident((1, c))],
        out_specs=blk,
        scratch_shapes=[pltpu.VMEM((LRU_TILE + SUBLANES, c), F32), pltpu.VMEM((LRU_TILE, c), F32),
                        pltpu.VMEM((LRU_TILE, c), F32), pltpu.VMEM((1, c), F32)],
        compiler_params=_params("parallel", "arbitrary"),
        name="rglru",
    )(xb.reshape(batch, seq, c), gb.reshape(batch, seq, c), conv_w, conv_b.reshape(1, c), wg, bg,
      lru_l.reshape(1, c), out_g.reshape(1, c))
    return out.reshape(batch * seq, c)


def _bias_kernel(rb_ref, onehot_ref, o_ref):
    o_ref[...] = jnp.dot(rb_ref[...], onehot_ref[...], preferred_element_type=F32,
                         precision=lax.Precision.HIGHEST)


def _t5_bucket(rel):
    n = jnp.maximum(rel, 0)
    max_exact = N_BUCKETS // 2
    nf = jnp.maximum(n, 1).astype(F32)
    large = max_exact + jnp.floor(jnp.log(nf / max_exact) / math.log(MAX_DISTANCE / max_exact)
                                  * (N_BUCKETS - max_exact)).astype(jnp.int32)
    large = jnp.minimum(large, N_BUCKETS - 1)
    return jnp.where(n < max_exact, n, large)


def _band_bias(rel_bias):
    i = jnp.arange(BLOCK)[:, None]
    j = jnp.arange(2 * BLOCK)[None, :]
    bucket = _t5_bucket(BLOCK + i - j).reshape(1, -1)
    onehot = (bucket == jnp.arange(N_BUCKETS)[:, None]).astype(F32)
    out = pl.pallas_call(
        _bias_kernel,
        out_shape=jax.ShapeDtypeStruct((N_HEADS, BLOCK * 2 * BLOCK), F32),
        name="band_bias",
    )(rel_bias.astype(F32).T, onehot)
    return out.reshape(N_HEADS, BLOCK, 2 * BLOCK)


def _swa_kernel(sink_ref, q_ref, kc_ref, kp_ref, vc_ref, vp_ref, bias_ref, qg_ref, kg_ref, og_ref, o_ref):
    kk = jnp.concatenate([kp_ref[...], kc_ref[...]], axis=0)
    vv = jnp.concatenate([vp_ref[...], vc_ref[...]], axis=0).astype(BF16)
    k_n = [_rms(kk[:, hk * HEAD_DIM:(hk + 1) * HEAD_DIM], kg_ref[...]).astype(BF16) for hk in range(N_KV_HEADS)]
    shape = (GROUP * BLOCK, 2 * BLOCK)
    qi = lax.broadcasted_iota(jnp.int32, shape, 0) % BLOCK
    kj = lax.broadcasted_iota(jnp.int32, shape, 1)
    for sub in range(SWA_BLOCKS):
        first_key = jnp.where(pl.program_id(1) > 0, 0, BLOCK) if sub == 0 else 0
        valid = (kj > jnp.maximum(qi + (BLOCK - WINDOW), first_key - 1)) & (kj <= qi + BLOCK)
        rows = slice(sub * BLOCK, (sub + 1) * BLOCK)
        keys = slice(sub * BLOCK, (sub + 2) * BLOCK)
        outs = []
        for hk in range(N_KV_HEADS):
            qs, sinks = [], []
            for g in range(GROUP):
                h = hk * GROUP + g
                qs.append(_rms(q_ref[rows, h * HEAD_DIM:(h + 1) * HEAD_DIM], qg_ref[...]).astype(BF16))
                sinks.append(jnp.full((BLOCK, 1), sink_ref[h], F32))
            q_n = jnp.concatenate(qs, axis=0)
            sink = jnp.concatenate(sinks, axis=0)
            s = lax.dot_general(q_n, k_n[hk][keys], (((1,), (1,)), ((), ())), preferred_element_type=F32)
            s = s * SCALE + bias_ref[hk * GROUP:(hk + 1) * GROUP].reshape(shape)
            s = jnp.where(valid, s, NEG_INF)
            m = jnp.maximum(jnp.max(s, axis=-1, keepdims=True), sink)
            p = jnp.exp(s - m)
            denom = jnp.sum(p, axis=-1, keepdims=True) + jnp.exp(sink - m)
            o = jnp.dot(p.astype(BF16), vv[keys, hk * HEAD_DIM:(hk + 1) * HEAD_DIM],
                        preferred_element_type=F32) / denom
            outs.extend(o[g * BLOCK:(g + 1) * BLOCK] for g in range(GROUP))
        y = jnp.concatenate(outs, axis=-1)
        o_ref[rows, :] = _rms(y, og_ref[...]).astype(o_ref.dtype)


def _swa(q, k, v, bias, q_norm_g, k_norm_g, sinks, out_g, batch, seq):
    nb = seq // BLOCK
    cur = lambda w: pl.BlockSpec((None, SWA_BLOCKS * BLOCK, w), lambda b, n: (b, n, 0))
    prev = lambda w: pl.BlockSpec((None, BLOCK, w), lambda b, n: (b, jnp.maximum(SWA_BLOCKS * n - 1, 0), 0))
    k3 = k.reshape(batch, seq, KV_WIDTH)
    v3 = v.reshape(batch, seq, KV_WIDTH)
    out = pl.pallas_call(
        _swa_kernel,
        out_shape=jax.ShapeDtypeStruct((batch, seq, ATTN_WIDTH), BF16),
        grid=(batch, nb // SWA_BLOCKS),
        in_specs=[pl.BlockSpec(memory_space=pltpu.SMEM), cur(ATTN_WIDTH), cur(KV_WIDTH), prev(KV_WIDTH),
                  cur(KV_WIDTH), prev(KV_WIDTH), _resident((N_HEADS, BLOCK, 2 * BLOCK)),
                  _resident((1, HEAD_DIM)), _resident((1, HEAD_DIM)), _resident((1, ATTN_WIDTH))],
        out_specs=cur(ATTN_WIDTH),
        compiler_params=_params("parallel", "parallel"),
        name="swa",
    )(sinks.astype(F32), q.reshape(batch, seq, ATTN_WIDTH), k3, k3, v3, v3, bias,
      q_norm_g.reshape(1, HEAD_DIM), k_norm_g.reshape(1, HEAD_DIM), out_g.reshape(1, ATTN_WIDTH))
    return out.reshape(batch * seq, ATTN_WIDTH)


def _outproj_kernel(x_ref, ml_ref, ma_ref, w_ref, g_ref, x1_ref, xn_ref):
    acc = jnp.dot(ml_ref[...], w_ref[:LRU_WIDTH, :], preferred_element_type=F32)
    acc = acc + jnp.dot(ma_ref[...], w_ref[LRU_WIDTH:, :], preferred_element_type=F32)
    x1 = x_ref[...] + acc
    x1_ref[...] = x1
    xn_ref[...] = _rms(x1, g_ref[...])


def _outproj(x2, y_lru, y_att, w_out, g):
    t = x2.shape[0]
    row = lambda w: pl.BlockSpec((ROW_TILE, w), lambda i: (i, 0))
    return pl.pallas_call(
        _outproj_kernel,
        out_shape=[jax.ShapeDtypeStruct((t, D_MODEL), F32)] * 2,
        grid=(t // ROW_TILE,),
        in_specs=[row(D_MODEL), row(LRU_WIDTH), row(ATTN_WIDTH), _resident((D_MODEL, D_MODEL)),
                  _resident((1, D_MODEL))],
        out_specs=[row(D_MODEL)] * 2,
        compiler_params=_params("parallel"),
        name="outproj",
    )(x2, y_lru, y_att, w_out.astype(BF16), g.reshape(1, D_MODEL))


def _pair_candidates(s0, i0, s1, i1):
    b_ids = lax.broadcasted_iota(jnp.int32, (SUBLANES,) + s1.shape[1:], 0)
    vals, ids = [], []
    a = 0
    while TOPK // (a + 1) > 1:
        n_b = TOPK // (a + 1)
        n_rows = -(-n_b // SUBLANES) * SUBLANES
        v = s0[a:a + 1] + s1[:n_rows]
        if n_b < n_rows:
            v = jnp.where(b_ids < n_b, v, -jnp.inf)
        vals.append(v)
        ids.append(i0[a:a + 1] * N_KEYS + i1[:n_rows])
        a += 1
    vals.append(s0[a:] + s1[0:1])
    ids.append(i0[a:] * N_KEYS + i1[0:1])
    return jnp.concatenate(vals, axis=0), jnp.concatenate(ids, axis=0)


def _top_rows(s, payload, k):
    n = s.shape[0]
    rows = lax.broadcasted_iota(jnp.int32, s.shape, 0).astype(F32)
    vals, picks = [], []
    for _ in range(k):
        m = jnp.max(s, axis=0, keepdims=True)
        first = jnp.min(jnp.where(s == m, rows, float(n)), axis=0, keepdims=True)
        hit = rows == first
        vals.append(m)
        if payload is None:
            picks.append(first)
        else:
            picks.append(jnp.max(jnp.where(hit, payload, -1.0), axis=0, keepdims=True))
        s = jnp.where(hit, -jnp.inf, s)
    return jnp.concatenate(vals, axis=0), jnp.concatenate(picks, axis=0)


def _peer_topk_kernel(xn_ref, wq_ref, keys_ref, off_ref, g_ref, q_scr, off_scr, g_scr):
    q_t = lax.dot_general(wq_ref[...], xn_ref[...].astype(BF16), (((1,), (1,)), ((), ())),
                          preferred_element_type=F32)
    q_scr[...] = q_t.astype(BF16)

    def head(h, carry):
        tops = []
        for half in range(2):
            hc = h * 2 + half
            q_hc = q_scr[pl.ds(pl.multiple_of(hc * D_HALF, D_HALF), D_HALF), :]
            s = jnp.dot(keys_ref[hc], q_hc, preferred_element_type=F32)
            tops.append(_top_rows(s, None, TOPK))
        (s0, i0), (s1, i1) = tops
        best, idx = _top_rows(*_pair_candidates(s0, i0, s1, i1), TOPK)
        e = jnp.exp(best - best[0:1])
        rows = pl.ds(pl.multiple_of(h * TOPK, TOPK), TOPK)
        off_scr[rows, :] = (idx * WORD_ROWS).astype(jnp.int32)
        g_scr[rows, :] = e / jnp.sum(e, axis=0, keepdims=True)
        return carry

    lax.fori_loop(0, PEER_HEADS, head, 0)
    off_ref[...] = off_scr[...].T
    g_ref[...] = g_scr[...].T


def _peer_topk(xn, w_query, sub_keys):
    t = xn.shape[0]
    nq = PEER_HEADS * D_QUERY
    out_blk = pl.BlockSpec((TOPK_TILE, N_PAIR), lambda i: (i, 0))
    return pl.pallas_call(
        _peer_topk_kernel,
        out_shape=[jax.ShapeDtypeStruct((t, N_PAIR), jnp.int32), jax.ShapeDtypeStruct((t, N_PAIR), F32)],
        grid=(t // TOPK_TILE,),
        in_specs=[pl.BlockSpec((TOPK_TILE, D_MODEL), lambda i: (i, 0)), _resident((nq, D_MODEL)),
                  _resident((2 * PEER_HEADS, N_KEYS, D_HALF))],
        out_specs=[out_blk, out_blk],
        scratch_shapes=[pltpu.VMEM((nq, TOPK_TILE), BF16), pltpu.VMEM((N_PAIR, TOPK_TILE), jnp.int32),
                        pltpu.VMEM((N_PAIR, TOPK_TILE), F32)],
        compiler_params=_params("parallel"),
        name="peer_topk",
    )(xn, w_query.T.astype(BF16), sub_keys.reshape(2 * PEER_HEADS, N_KEYS, D_HALF).astype(BF16))


def _pack_kernel(x_ref, o_ref):
    rows = x_ref.shape[0]
    for j in range(WORD_ROWS):
        lo = x_ref[:, j * LANES:(j + 1) * LANES]
        hi = x_ref[:, (WORD_ROWS + j) * LANES:(WORD_ROWS + j + 1) * LANES]
        o_ref[pl.ds(j, rows, stride=WORD_ROWS), :] = pltpu.pack_elementwise([lo, hi], packed_dtype=BF16)


def _pack_table(tab):
    n = tab.shape[0]
    return pl.pallas_call(
        _pack_kernel,
        out_shape=jax.ShapeDtypeStruct((n * WORD_ROWS, LANES), PACKED),
        grid=(n // ROW_TILE,),
        in_specs=[pl.BlockSpec((ROW_TILE, D_MODEL), lambda i: (i, 0))],
        out_specs=pl.BlockSpec((ROW_TILE * WORD_ROWS, LANES), lambda i: (i, 0)),
        compiler_params=_params("parallel"),
        name="pack_table",
    )(tab)


def _slot_chunk(slot):
    return slot // 2 + WORD_ROWS * (slot % 2)


def _chunk_sum_matrix():
    return (np.arange(D_MODEL)[:, None] // N_CHUNK == np.arange(N_PAIR)[None, :]).astype(np.float32)


def _gathered_tile(off_ref, tab_ref, t, j):
    parts = [tab_ref[pl.ds(pl.multiple_of(off_ref[t, TILE_PAIRS * j + i], WORD_ROWS), WORD_ROWS), :]
             for i in range(TILE_PAIRS)]
    return pltpu.bitcast(jnp.concatenate(parts, axis=0), BF16)


def _select_dot(a, sel):
    out = None
    for _ in range(3):
        piece = a.astype(BF16)
        part = jnp.dot(piece, sel, preferred_element_type=F32)
        out = part if out is None else out + part
        a = a - piece.astype(F32)
    return out


def _split_bf16(a):
    hi = a.astype(BF16)
    return jnp.concatenate([hi, (a - hi.astype(F32)).astype(BF16)], axis=0)


def _for_each_token(off_now, off_next, off_bufs, sem, compute):
    s = pl.program_id(0)

    def fetch(src, which):
        return pltpu.make_async_copy(src.at[pl.ds(which * PEER_HALF, PEER_HALF)], off_bufs[which], sem.at[which])

    def consume(which):
        for t in range(PEER_HALF):
            compute(which * PEER_HALF + t, off_bufs[which], t)

    not_last = s + 1 < pl.num_programs(0)

    @pl.when(s == 0)
    def _():
        first = fetch(off_now, 0)
        first.start()
        first.wait()

    fetch(off_now, 1).start()
    consume(0)
    fetch(off_now, 1).wait()

    @pl.when(not_last)
    def _():
        fetch(off_next, 0).start()

    consume(1)

    @pl.when(not_last)
    def _():
        fetch(off_next, 0).wait()


def _group_row0():
    return pl.multiple_of((pl.program_id(0) % STEPS_PER_GROUP) * PEER_STEP, PEER_STEP)


def _peer_u_kernel(off_now, off_next, x_ref, tab_ref, sum_ref, gate_ref, w_ref, off_a, off_b, sem, z_scr):
    shape = (2 * N_CHUNK, MXU_ROWS)
    own = (lax.broadcasted_iota(jnp.int32, shape, 0) % N_CHUNK) == _slot_chunk(lax.broadcasted_iota(jnp.int32, shape, 1) % N_CHUNK)
    row0 = _group_row0()

    def compute(tok, off_ref, t):
        lhs = _split_bf16(x_ref[tok])
        zs = []
        for j in range(N_TILES):
            o = lax.dot_general(lhs, _gathered_tile(off_ref, tab_ref, t, j), (((1,), (1,)), ((), ())),
                                preferred_element_type=F32)
            zs.append(jnp.sum(jnp.where(own, o, 0.0), axis=0, keepdims=True))
        z_scr[pl.ds(row0 + tok, 1), :] = jnp.concatenate(zs, axis=1)

    _for_each_token(off_now, off_next, (off_a, off_b), sem, compute)

    @pl.when(pl.program_id(0) % STEPS_PER_GROUP == STEPS_PER_GROUP - 1)
    def _():
        act = _select_dot(z_scr[...], sum_ref[...])
        w_ref[...] = gate_ref[...] * _gelu(act)


def _peer_v_kernel(off_now, off_next, w_ref, x1_ref, tab_ref, spread_ref, y_ref, off_a, off_b, sem, wx_scr):
    shape = (N_CHUNK, D_MODEL)
    own = lax.broadcasted_iota(jnp.int32, shape, 0) == _slot_chunk(lax.broadcasted_iota(jnp.int32, shape, 1) % N_CHUNK)
    row0 = _group_row0()

    @pl.when(pl.program_id(0) % STEPS_PER_GROUP == 0)
    def _():
        wx_scr[...] = _select_dot(w_ref[...], spread_ref[...])

    def compute(tok, off_ref, t):
        lhs = _split_bf16(jnp.where(own, jnp.broadcast_to(wx_scr[pl.ds(row0 + tok, 1), :], shape), 0.0))
        r = None
        for j in range(N_TILES):
            part = jnp.dot(lhs[:, j * MXU_ROWS:(j + 1) * MXU_ROWS], _gathered_tile(off_ref, tab_ref, t, j),
                           preferred_element_type=F32)
            r = part if r is None else r + part
        y_ref[tok] = x1_ref[tok] + r[:N_CHUNK] + r[N_CHUNK:]

    _for_each_token(off_now, off_next, (off_a, off_b), sem, compute)


def _peer_specs(n_steps):
    pairs = pl.BlockSpec((PEER_GROUP, N_PAIR), lambda i: (i // STEPS_PER_GROUP, 0))
    rows = pl.BlockSpec((PEER_STEP, N_CHUNK, LANES), lambda i: (i, 0, 0))
    table = _resident((N_EXPERTS * WORD_ROWS, LANES))
    scratch = ([pltpu.SMEM((PEER_HALF, N_PAIR), jnp.int32)] * 2 + [pltpu.SemaphoreType.DMA((2,))]
               + [pltpu.VMEM((PEER_GROUP, D_MODEL), F32)])
    offs = [pl.BlockSpec((PEER_STEP, N_PAIR), lambda i: (i, 0)),
            pl.BlockSpec((PEER_STEP, N_PAIR), lambda i: (jnp.minimum(i + 1, n_steps - 1), 0))]
    return offs, pairs, rows, table, scratch


def _peer_u(off, gate, xn, tab_u):
    t = off.shape[0]
    offs, pairs, rows, table, scratch = _peer_specs(t // PEER_STEP)
    return pl.pallas_call(
        _peer_u_kernel,
        out_shape=jax.ShapeDtypeStruct((t, N_PAIR), F32),
        grid=(t // PEER_STEP,),
        in_specs=offs + [rows, table, _resident((D_MODEL, N_PAIR)), pairs],
        out_specs=pairs,
        scratch_shapes=scratch,
        compiler_params=_params("arbitrary"),
        name="peer_u",
    )(off, off, xn.reshape(t, N_CHUNK, LANES), tab_u, jnp.asarray(_chunk_sum_matrix(), BF16), gate)


def _peer_v(off, w, x1, tab_v):
    t = off.shape[0]
    offs, pairs, rows, table, scratch = _peer_specs(t // PEER_STEP)
    y = pl.pallas_call(
        _peer_v_kernel,
        out_shape=jax.ShapeDtypeStruct((t, N_CHUNK, LANES), F32),
        grid=(t // PEER_STEP,),
        in_specs=offs + [pairs, rows, table, _resident((N_PAIR, D_MODEL))],
        out_specs=rows,
        scratch_shapes=scratch,
        compiler_params=_params("arbitrary"),
        name="peer_v",
    )(off, off, w, x1.reshape(t, N_CHUNK, LANES), tab_v, jnp.asarray(_chunk_sum_matrix().T, BF16))
    return y.reshape(t, D_MODEL)


def kernel(x, ln_mix_g, w_in, conv_w, conv_b, w_gate_a, b_gate_a, w_gate_x, b_gate_x, lru_L, q_norm_g,
           k_norm_g, sinks, lru_out_g, attn_out_g, w_out, ln_ffn_g, w_query, sub_keys, expert_u, expert_v,
           rel_bias):
    batch, seq, _ = x.shape
    bias = _band_bias(rel_bias)
    x2 = x.reshape(batch * seq, D_MODEL)
    for l in range(w_in.shape[0]):
        xb, gb, q, k, v = _inproj(x2, ln_mix_g[l], w_in[l])
        y_lru = _rglru(xb, gb, conv_w[l], conv_b[l], w_gate_a[l], b_gate_a[l], w_gate_x[l], b_gate_x[l],
                       lru_L[l], lru_out_g[l], batch, seq)
        y_att = _swa(q, k, v, bias, q_norm_g[l], k_norm_g[l], sinks[l], attn_out_g[l], batch, seq)
        x1, xn = _outproj(x2, y_lru, y_att, w_out[l], ln_ffn_g[l])
        off, gate = _peer_topk(xn, w_query[l], sub_keys[l])
        w = _peer_u(off, gate, xn, _pack_table(expert_u[l]))
        x2 = _peer_v(off, w, x1, _pack_table(expert_v[l]))
    return x2.reshape(batch, seq, D_MODEL)
```

```python
import math

import jax
import jax.numpy as jnp
import numpy as np
from jax import lax
from jax.experimental import pallas as pl
from jax.experimental.pallas import tpu as pltpu

D_MODEL = 1024
LRU_WIDTH = 512
LRU_BLOCKS = 8
LRU_BLOCK = LRU_WIDTH // LRU_BLOCKS
CONV_WIDTH = 4
LRU_C = 8.0
N_HEADS = 8
N_KV_HEADS = 2
GROUP = N_HEADS // N_KV_HEADS
HEAD_DIM = 64
ATTN_WIDTH = N_HEADS * HEAD_DIM
KV_WIDTH = N_KV_HEADS * HEAD_DIM
WINDOW = 128
BLOCK = 128
N_BUCKETS = 32
MAX_DISTANCE = 128
PEER_HEADS = 8
N_KEYS = 128
N_EXPERTS = N_KEYS * N_KEYS
D_QUERY = 256
D_HALF = D_QUERY // 2
TOPK = 16
N_PAIR = PEER_HEADS * TOPK
IN_COLS = 2 * LRU_WIDTH + ATTN_WIDTH + 2 * KV_WIDTH
EPS = 1e-6
NEG_INF = -1e30
SCALE = HEAD_DIM ** -0.5

F32 = jnp.float32
BF16 = jnp.bfloat16
PACKED = jnp.uint32
LANES = 128
SUBLANES = 8
N_CHUNK = D_MODEL // LANES
WORD_ROWS = N_CHUNK // 2
MXU_ROWS = 256
TILE_PAIRS = MXU_ROWS // N_CHUNK
N_TILES = N_PAIR // TILE_PAIRS
VMEM_LIMIT = 52 * 1024 * 1024

ROW_TILE = 512
PACK_TILE = 2048
LRU_TILE = 256
SWA_BLOCKS = 2
TOPK_TILE = 512
PEER_HALF = 64
PEER_STEP = 2 * PEER_HALF
PEER_GROUP = 256
STEPS_PER_GROUP = PEER_GROUP // PEER_STEP


def _rms(x, g):
    return x * lax.rsqrt(jnp.mean(x * x, axis=-1, keepdims=True) + EPS) * g


def _gelu(x):
    return 0.5 * x * (1.0 + jnp.tanh(math.sqrt(2.0 / math.pi) * (x + 0.044715 * (x * x * x))))


def _params(*sem):
    return pltpu.CompilerParams(dimension_semantics=sem, vmem_limit_bytes=VMEM_LIMIT)


def _resident(shape):
    zeros = (0,) * len(shape)
    return pl.BlockSpec(shape, lambda *_: zeros, pipeline_mode=pl.Buffered(1))


def _inproj_kernel(x_ref, g_ref, w_ref, xb_ref, gb_ref, q_ref, k_ref, v_ref):
    h = _rms(x_ref[...], g_ref[...])
    p = jnp.dot(h.astype(BF16), w_ref[...], preferred_element_type=F32)
    c0, c1, c2, c3 = LRU_WIDTH, 2 * LRU_WIDTH, 2 * LRU_WIDTH + ATTN_WIDTH, IN_COLS - KV_WIDTH
    xb_ref[...] = p[:, :c0]
    gb_ref[...] = p[:, c0:c1]
    q_ref[...] = p[:, c1:c2]
    k_ref[...] = p[:, c2:c3]
    v_ref[...] = p[:, c3:]


def _inproj(x2, g, w_in):
    t = x2.shape[0]
    widths = (LRU_WIDTH, LRU_WIDTH, ATTN_WIDTH, KV_WIDTH, KV_WIDTH)
    row = lambda w: pl.BlockSpec((ROW_TILE, w), lambda i: (i, 0))
    return pl.pallas_call(
        _inproj_kernel,
        out_shape=[jax.ShapeDtypeStruct((t, w), F32) for w in widths],
        grid=(t // ROW_TILE,),
        in_specs=[row(D_MODEL), _resident((1, D_MODEL)), _resident((D_MODEL, IN_COLS))],
        out_specs=[row(w) for w in widths],
        compiler_params=_params("parallel"),
        name="inproj",
    )(x2, g.reshape(1, D_MODEL), w_in.astype(BF16))


def _rglru_kernel(xb_ref, gb_ref, cw_ref, cb_ref, wg_ref, bg_ref, l_ref, og_ref, o_ref,
                  xs_scr, a_scr, b_scr, h_scr):
    tt, c = xb_ref.shape

    @pl.when(pl.program_id(1) == 0)
    def _():
        xs_scr[0:SUBLANES, :] = jnp.zeros((SUBLANES, c), F32)
        h_scr[...] = jnp.zeros_like(h_scr)

    xb = xb_ref[...]
    xs_scr[SUBLANES:SUBLANES + tt, :] = xb
    xc = cb_ref[...] + xb * cw_ref[CONV_WIDTH - 1:CONV_WIDTH, :]
    for back in range(1, CONV_WIDTH):
        tap = CONV_WIDTH - 1 - back
        xc = xc + xs_scr[SUBLANES - back:SUBLANES - back + tt, :] * cw_ref[tap:tap + 1, :]
    xs_scr[0:SUBLANES, :] = xb[tt - SUBLANES:, :]

    gates = jnp.dot(xc.astype(BF16), wg_ref[...], preferred_element_type=F32) + bg_ref[...]
    r = jax.nn.sigmoid(gates[:, :c])
    ig = jax.nn.sigmoid(gates[:, c:])
    lam = l_ref[...]
    softplus_neg = jnp.maximum(-lam, 0.0) + jnp.log1p(jnp.exp(-jnp.abs(lam)))
    log_a = (-LRU_C) * r * softplus_neg
    a = jnp.exp(log_a)
    b = jnp.sqrt(-jnp.tanh(log_a) * (a * a + 1.0)) * (ig * xc)
    a_scr[...] = a
    b_scr[...] = b

    rows = lax.broadcasted_iota(jnp.int32, (SUBLANES, c), 0)

    def tile(i, h):
        off = pl.multiple_of(i * SUBLANES, SUBLANES)
        at = a_scr[pl.ds(off, SUBLANES), :]
        bt = b_scr[pl.ds(off, SUBLANES), :]
        for d in (1, 2, 4):
            keep = rows >= d
            bt = jnp.where(keep, at * pltpu.roll(bt, d, axis=0) + bt, bt)
            at = jnp.where(keep, at * pltpu.roll(at, d, axis=0), at)
        ht = at * h + bt
        b_scr[pl.ds(off, SUBLANES), :] = ht
        return ht[SUBLANES - 1:SUBLANES, :]

    h_scr[...] = lax.fori_loop(0, tt // SUBLANES, tile, h_scr[...])
    y = b_scr[...] * _gelu(gb_ref[...])
    o_ref[...] = _rms(y, og_ref[...]).astype(o_ref.dtype)


def _rglru(xb, gb, conv_w, conv_b, w_gate_a, b_gate_a, w_gate_x, b_gate_x, lru_l, out_g, batch, seq):
    c = LRU_WIDTH
    eye = jnp.eye(LRU_BLOCKS, dtype=F32)
    dense = lambda w: jnp.einsum("nij,nm->nimj", w, eye).reshape(c, c)
    wg = jnp.concatenate([dense(w_gate_a), dense(w_gate_x)], axis=1).astype(BF16)
    bg = jnp.concatenate([b_gate_a.reshape(1, c), b_gate_x.reshape(1, c)], axis=1)
    blk = pl.BlockSpec((None, LRU_TILE, c), lambda b, j: (b, j, 0))
    out = pl.pallas_call(
        _rglru_kernel,
        out_shape=jax.ShapeDtypeStruct((batch, seq, c), BF16),
        grid=(batch, seq // LRU_TILE),
        in_specs=[blk, blk, _resident((CONV_WIDTH, c)), _resident((1, c)), _resident((c, 2 * c)),
                  _resident((1, 2 * c)), _resident((1, c)), _resident((1, c))],
        out_specs=blk,
        scratch_shapes=[pltpu.VMEM((LRU_TILE + SUBLANES, c), F32), pltpu.VMEM((LRU_TILE, c), F32),
                        pltpu.VMEM((LRU_TILE, c), F32), pltpu.VMEM((1, c), F32)],
        compiler_params=_params("parallel", "arbitrary"),
        name="rglru",
    )(xb.reshape(batch, seq, c), gb.reshape(batch, seq, c), conv_w, conv_b.reshape(1, c), wg, bg,
      lru_l.reshape(1, c), out_g.reshape(1, c))
    return out.reshape(batch * seq, c)


def _bias_kernel(rb_ref, onehot_ref, o_ref):
    o_ref[...] = jnp.dot(rb_ref[...], onehot_ref[...], preferred_element_type=F32,
                         precision=lax.Precision.HIGHEST)


def _t5_bucket(rel):
    n = jnp.maximum(rel, 0)
    max_exact = N_BUCKETS // 2
    nf = jnp.maximum(n, 1).astype(F32)
    large = max_exact + jnp.floor(jnp.log(nf / max_exact) / math.log(MAX_DISTANCE / max_exact)
                                  * (N_BUCKETS - max_exact)).astype(jnp.int32)
    large = jnp.minimum(large, N_BUCKETS - 1)
    return jnp.where(n < max_exact, n, large)


def _band_bias(rel_bias):
    i = jnp.arange(BLOCK)[:, None]
    j = jnp.arange(2 * BLOCK)[None, :]
    bucket = _t5_bucket(BLOCK + i - j).reshape(1, -1)
    onehot = (bucket == jnp.arange(N_BUCKETS)[:, None]).astype(F32)
    out = pl.pallas_call(
        _bias_kernel,
        out_shape=jax.ShapeDtypeStruct((N_HEADS, BLOCK * 2 * BLOCK), F32),
        name="band_bias",
    )(rel_bias.astype(F32).T, onehot)
    return out.reshape(N_HEADS, BLOCK, 2 * BLOCK)


def _swa_kernel(sink_ref, q_ref, kc_ref, kp_ref, vc_ref, vp_ref, bias_ref, qg_ref, kg_ref, og_ref, o_ref):
    kk = jnp.concatenate([kp_ref[...], kc_ref[...]], axis=0)
    vv = jnp.concatenate([vp_ref[...], vc_ref[...]], axis=0).astype(BF16)
    k_n = [_rms(kk[:, hk * HEAD_DIM:(hk + 1) * HEAD_DIM], kg_ref[...]).astype(BF16) for hk in range(N_KV_HEADS)]
    shape = (GROUP * BLOCK, 2 * BLOCK)
    qi = lax.broadcasted_iota(jnp.int32, shape, 0) % BLOCK
    kj = lax.broadcasted_iota(jnp.int32, shape, 1)
    for sub in range(SWA_BLOCKS):
        first_key = jnp.where(pl.program_id(1) > 0, 0, BLOCK) if sub == 0 else 0
        valid = (kj > jnp.maximum(qi + (BLOCK - WINDOW), first_key - 1)) & (kj <= qi + BLOCK)
        rows = slice(sub * BLOCK, (sub + 1) * BLOCK)
        keys = slice(sub * BLOCK, (sub + 2) * BLOCK)
        outs = []
        for hk in range(N_KV_HEADS):
            qs, sinks = [], []
            for g in range(GROUP):
                h = hk * GROUP + g
                qs.append(_rms(q_ref[rows, h * HEAD_DIM:(h + 1) * HEAD_DIM], qg_ref[...]).astype(BF16))
                sinks.append(jnp.full((BLOCK, 1), sink_ref[h], F32))
            q_n = jnp.concatenate(qs, axis=0)
            sink = jnp.concatenate(sinks, axis=0)
            s = lax.dot_general(q_n, k_n[hk][keys], (((1,), (1,)), ((), ())), preferred_element_type=F32)
            s = s * SCALE + bias_ref[hk * GROUP:(hk + 1) * GROUP].reshape(shape)
            s = jnp.where(valid, s, NEG_INF)
            m = jnp.maximum(jnp.max(s, axis=-1, keepdims=True), sink)
            p = jnp.exp(s - m)
            denom = jnp.sum(p, axis=-1, keepdims=True) + jnp.exp(sink - m)
            o = jnp.dot(p.astype(BF16), vv[keys, hk * HEAD_DIM:(hk + 1) * HEAD_DIM],
                        preferred_element_type=F32) / denom
            outs.extend(o[g * BLOCK:(g + 1) * BLOCK] for g in range(GROUP))
        y = jnp.concatenate(outs, axis=-1)
        o_ref[rows, :] = _rms(y, og_ref[...]).astype(o_ref.dtype)


def _swa(q, k, v, bias, q_norm_g, k_norm_g, sinks, out_g, batch, seq):
    nb = seq // BLOCK
    cur = lambda w: pl.BlockSpec((None, SWA_BLOCKS * BLOCK, w), lambda b, n: (b, n, 0))
    prev = lambda w: pl.BlockSpec((None, BLOCK, w), lambda b, n: (b, jnp.maximum(SWA_BLOCKS * n - 1, 0), 0))
    k3 = k.reshape(batch, seq, KV_WIDTH)
    v3 = v.reshape(batch, seq, KV_WIDTH)
    out = pl.pallas_call(
        _swa_kernel,
        out_shape=jax.ShapeDtypeStruct((batch, seq, ATTN_WIDTH), BF16),
        grid=(batch, nb // SWA_BLOCKS),
        in_specs=[pl.BlockSpec(memory_space=pltpu.SMEM), cur(ATTN_WIDTH), cur(KV_WIDTH), prev(KV_WIDTH),
                  cur(KV_WIDTH), prev(KV_WIDTH), _resident((N_HEADS, BLOCK, 2 * BLOCK)),
                  _resident((1, HEAD_DIM)), _resident((1, HEAD_DIM)), _resident((1, ATTN_WIDTH))],
        out_specs=cur(ATTN_WIDTH),
        compiler_params=_params("parallel", "parallel"),
        name="swa",
    )(sinks.astype(F32), q.reshape(batch, seq, ATTN_WIDTH), k3, k3, v3, v3, bias,
      q_norm_g.reshape(1, HEAD_DIM), k_norm_g.reshape(1, HEAD_DIM), out_g.reshape(1, ATTN_WIDTH))
    return out.reshape(batch * seq, ATTN_WIDTH)


def _outproj_kernel(x_ref, ml_ref, ma_ref, w_ref, g_ref, x1_ref, xn_ref):
    acc = jnp.dot(ml_ref[...], w_ref[:LRU_WIDTH, :], preferred_element_type=F32)
    acc = acc + jnp.dot(ma_ref[...], w_ref[LRU_WIDTH:, :], preferred_element_type=F32)
    x1 = x_ref[...] + acc
    x1_ref[...] = x1
    xn_ref[...] = _rms(x1, g_ref[...])


def _outproj(x2, y_lru, y_att, w_out, g):
    t = x2.shape[0]
    row = lambda w: pl.BlockSpec((ROW_TILE, w), lambda i: (i, 0))
    return pl.pallas_call(
        _outproj_kernel,
        out_shape=[jax.ShapeDtypeStruct((t, D_MODEL), F32)] * 2,
        grid=(t // ROW_TILE,),
        in_specs=[row(D_MODEL), row(LRU_WIDTH), row(ATTN_WIDTH), _resident((D_MODEL, D_MODEL)),
                  _resident((1, D_MODEL))],
        out_specs=[row(D_MODEL)] * 2,
        compiler_params=_params("parallel"),
        name="outproj",
    )(x2, y_lru, y_att, w_out.astype(BF16), g.reshape(1, D_MODEL))


def _pair_candidates(s0, i0, s1, i1):
    b_ids = lax.broadcasted_iota(jnp.int32, (SUBLANES,) + s1.shape[1:], 0)
    vals, ids = [], []
    a = 0
    while TOPK // (a + 1) > 1:
        n_b = TOPK // (a + 1)
        n_rows = -(-n_b // SUBLANES) * SUBLANES
        v = s0[a:a + 1] + s1[:n_rows]
        if n_b < n_rows:
            v = jnp.where(b_ids < n_b, v, -jnp.inf)
        vals.append(v)
        ids.append(i0[a:a + 1] * N_KEYS + i1[:n_rows])
        a += 1
    vals.append(s0[a:] + s1[0:1])
    ids.append(i0[a:] * N_KEYS + i1[0:1])
    return jnp.concatenate(vals, axis=0), jnp.concatenate(ids, axis=0)


def _top_rows(s, payload, k):
    n = s.shape[0]
    rows = lax.broadcasted_iota(jnp.int32, s.shape, 0).astype(F32)
    vals, picks = [], []
    for _ in range(k):
        m = jnp.max(s, axis=0, keepdims=True)
        first = jnp.min(jnp.where(s == m, rows, float(n)), axis=0, keepdims=True)
        hit = rows == first
        vals.append(m)
        if payload is None:
            picks.append(first)
        else:
            picks.append(jnp.max(jnp.where(hit, payload, -1.0), axis=0, keepdims=True))
        s = jnp.where(hit, -jnp.inf, s)
    return jnp.concatenate(vals, axis=0), jnp.concatenate(picks, axis=0)


def _peer_topk_kernel(xn_ref, wq_ref, keys_ref, off_ref, g_ref, q_scr, off_scr, g_scr):
    q_t = lax.dot_general(wq_ref[...], xn_ref[...].astype(BF16), (((1,), (1,)), ((), ())),
                          preferred_element_type=F32)
    q_scr[...] = q_t.astype(BF16)

    def head(h, carry):
        tops = []
        for half in range(2):
            hc = h * 2 + half
            q_hc = q_scr[pl.ds(pl.multiple_of(hc * D_HALF, D_HALF), D_HALF), :]
            s = jnp.dot(keys_ref[hc], q_hc, preferred_element_type=F32)
            tops.append(_top_rows(s, None, TOPK))
        (s0, i0), (s1, i1) = tops
        best, idx = _top_rows(*_pair_candidates(s0, i0, s1, i1), TOPK)
        e = jnp.exp(best - best[0:1])
        rows = pl.ds(pl.multiple_of(h * TOPK, TOPK), TOPK)
        off_scr[rows, :] = (idx * WORD_ROWS).astype(jnp.int32)
        g_scr[rows, :] = e / jnp.sum(e, axis=0, keepdims=True)
        return carry

    lax.fori_loop(0, PEER_HEADS, head, 0)
    off_ref[...] = off_scr[...].T
    g_ref[...] = g_scr[...].T


def _peer_topk(xn, w_query, sub_keys):
    t = xn.shape[0]
    nq = PEER_HEADS * D_QUERY
    out_blk = pl.BlockSpec((TOPK_TILE, N_PAIR), lambda i: (i, 0))
    return pl.pallas_call(
        _peer_topk_kernel,
        out_shape=[jax.ShapeDtypeStruct((t, N_PAIR), jnp.int32), jax.ShapeDtypeStruct((t, N_PAIR), F32)],
        grid=(t // TOPK_TILE,),
        in_specs=[pl.BlockSpec((TOPK_TILE, D_MODEL), lambda i: (i, 0)), _resident((nq, D_MODEL)),
                  _resident((2 * PEER_HEADS, N_KEYS, D_HALF))],
        out_specs=[out_blk, out_blk],
        scratch_shapes=[pltpu.VMEM((nq, TOPK_TILE), BF16), pltpu.VMEM((N_PAIR, TOPK_TILE), jnp.int32),
                        pltpu.VMEM((N_PAIR, TOPK_TILE), F32)],
        compiler_params=_params("parallel"),
        name="peer_topk",
    )(xn, w_query.T.astype(BF16), sub_keys.reshape(2 * PEER_HEADS, N_KEYS, D_HALF).astype(BF16))


def _pack_kernel(x_ref, o_ref):
    rows = x_ref.shape[0]
    for j in range(WORD_ROWS):
        lo = x_ref[:, j * LANES:(j + 1) * LANES]
        hi = x_ref[:, (WORD_ROWS + j) * LANES:(WORD_ROWS + j + 1) * LANES]
        o_ref[pl.ds(j, rows, stride=WORD_ROWS), :] = pltpu.pack_elementwise([lo, hi], packed_dtype=BF16)


def _pack_table(tab):
    n = tab.shape[0]
    return pl.pallas_call(
        _pack_kernel,
        out_shape=jax.ShapeDtypeStruct((n * WORD_ROWS, LANES), PACKED),
        grid=(n // PACK_TILE,),
        in_specs=[pl.BlockSpec((PACK_TILE, D_MODEL), lambda i: (i, 0))],
        out_specs=pl.BlockSpec((PACK_TILE * WORD_ROWS, LANES), lambda i: (i, 0)),
        compiler_params=_params("parallel"),
        name="pack_table",
    )(tab)


def _slot_chunk(slot):
    return slot // 2 + WORD_ROWS * (slot % 2)


def _chunk_sum_matrix():
    return (np.arange(D_MODEL)[:, None] // N_CHUNK == np.arange(N_PAIR)[None, :]).astype(np.float32)


def _gathered_tile(off_ref, tab_ref, t, j):
    parts = [tab_ref[pl.ds(pl.multiple_of(off_ref[t, TILE_PAIRS * j + i], WORD_ROWS), WORD_ROWS), :]
             for i in range(TILE_PAIRS)]
    return pltpu.bitcast(jnp.concatenate(parts, axis=0), BF16)


def _select_dot(a, sel):
    out = None
    for _ in range(3):
        piece = a.astype(BF16)
        part = jnp.dot(piece, sel, preferred_element_type=F32)
        out = part if out is None else out + part
        a = a - piece.astype(F32)
    return out


def _swap_tile_rows(tiles):
    tiles = list(tiles)
    rows = lax.broadcasted_iota(jnp.int32, tiles[0].shape, 0)
    d = SUBLANES // 2
    while d:
        low = (rows & d) == 0
        out = list(tiles)
        for i in range(SUBLANES):
            if not i & d:
                a, b = tiles[i], tiles[i + d]
                out[i] = jnp.where(low, a, pltpu.roll(b, d, axis=0))
                out[i + d] = jnp.where(low, pltpu.roll(a, SUBLANES - d, axis=0), b)
        tiles = out
        d //= 2
    return tiles


def _split_bf16(a):
    hi = a.astype(BF16)
    return jnp.concatenate([hi, (a - hi.astype(F32)).astype(BF16)], axis=0)


def _for_each_token(off_now, off_next, off_bufs, sem, compute):
    s = pl.program_id(0)

    def fetch(src, which):
        return pltpu.make_async_copy(src.at[pl.ds(which * PEER_HALF, PEER_HALF)], off_bufs[which], sem.at[which])

    def consume(which):
        for t in range(PEER_HALF):
            compute(which * PEER_HALF + t, off_bufs[which], t)

    not_last = s + 1 < pl.num_programs(0)

    @pl.when(s == 0)
    def _():
        first = fetch(off_now, 0)
        first.start()
        first.wait()

    fetch(off_now, 1).start()
    consume(0)
    fetch(off_now, 1).wait()

    @pl.when(not_last)
    def _():
        fetch(off_next, 0).start()

    consume(1)

    @pl.when(not_last)
    def _():
        fetch(off_next, 0).wait()


def _group_row0():
    return pl.multiple_of((pl.program_id(0) % STEPS_PER_GROUP) * PEER_STEP, PEER_STEP)


def _peer_u_kernel(off_now, off_next, x_ref, tab_ref, sum_ref, gate_ref, w_ref, off_a, off_b, sem, z_scr):
    shape = (2 * N_CHUNK, MXU_ROWS)
    own = (lax.broadcasted_iota(jnp.int32, shape, 0) % N_CHUNK) == _slot_chunk(lax.broadcasted_iota(jnp.int32, shape, 1) % N_CHUNK)
    row0 = _group_row0()

    def compute(tok, off_ref, t):
        lhs = _split_bf16(x_ref[tok])
        zs = []
        for j in range(N_TILES):
            o = lax.dot_general(lhs, _gathered_tile(off_ref, tab_ref, t, j), (((1,), (1,)), ((), ())),
                                preferred_element_type=F32)
            zs.append(jnp.sum(jnp.where(own, o, 0.0), axis=0, keepdims=True))
        z_scr[pl.ds(row0 + tok, 1), :] = jnp.concatenate(zs, axis=1)

    _for_each_token(off_now, off_next, (off_a, off_b), sem, compute)

    @pl.when(pl.program_id(0) % STEPS_PER_GROUP == STEPS_PER_GROUP - 1)
    def _():
        act = _select_dot(z_scr[...], sum_ref[...])
        w_ref[...] = gate_ref[...] * _gelu(act)


def _peer_v_kernel(off_now, off_next, w_ref, x1_ref, tab_ref, spread_ref, y_ref, off_a, off_b, sem, wx_scr):
    shape = (N_CHUNK, D_MODEL)
    own = lax.broadcasted_iota(jnp.int32, shape, 0) == _slot_chunk(lax.broadcasted_iota(jnp.int32, shape, 1) % N_CHUNK)
    row0 = _group_row0()

    @pl.when(pl.program_id(0) % STEPS_PER_GROUP == 0)
    def _():
        wx_scr[...] = _select_dot(w_ref[...], spread_ref[...])

    def compute(tok, off_ref, t):
        lhs = _split_bf16(jnp.where(own, jnp.broadcast_to(wx_scr[pl.ds(row0 + tok, 1), :], shape), 0.0))
        r = None
        for j in range(N_TILES):
            part = jnp.dot(lhs[:, j * MXU_ROWS:(j + 1) * MXU_ROWS], _gathered_tile(off_ref, tab_ref, t, j),
                           preferred_element_type=F32)
            r = part if r is None else r + part
        pending.append(r[:N_CHUNK] + r[N_CHUNK:])
        if len(pending) == SUBLANES:
            by_chunk = _swap_tile_rows(pending)
            pending.clear()
            toks = slice(tok + 1 - SUBLANES, tok + 1)
            for c in range(N_CHUNK):
                cols = slice(c * LANES, (c + 1) * LANES)
                y_ref[toks, cols] = x1_ref[toks, cols] + by_chunk[c]

    pending = []
    _for_each_token(off_now, off_next, (off_a, off_b), sem, compute)


def _peer_specs(n_steps):
    pairs = pl.BlockSpec((PEER_GROUP, N_PAIR), lambda i: (i // STEPS_PER_GROUP, 0))
    rows = pl.BlockSpec((PEER_STEP, N_CHUNK, LANES), lambda i: (i, 0, 0))
    table = _resident((N_EXPERTS * WORD_ROWS, LANES))
    scratch = ([pltpu.SMEM((PEER_HALF, N_PAIR), jnp.int32)] * 2 + [pltpu.SemaphoreType.DMA((2,))]
               + [pltpu.VMEM((PEER_GROUP, D_MODEL), F32)])
    offs = [pl.BlockSpec((PEER_STEP, N_PAIR), lambda i: (i, 0)),
            pl.BlockSpec((PEER_STEP, N_PAIR), lambda i: (jnp.minimum(i + 1, n_steps - 1), 0))]
    return offs, pairs, rows, table, scratch


def _peer_u(off, gate, xn, tab_u):
    t = off.shape[0]
    offs, pairs, rows, table, scratch = _peer_specs(t // PEER_STEP)
    return pl.pallas_call(
        _peer_u_kernel,
        out_shape=jax.ShapeDtypeStruct((t, N_PAIR), F32),
        grid=(t // PEER_STEP,),
        in_specs=offs + [rows, table, _resident((D_MODEL, N_PAIR)), pairs],
        out_specs=pairs,
        scratch_shapes=scratch,
        compiler_params=_params("arbitrary"),
        name="peer_u",
    )(off, off, xn.reshape(t, N_CHUNK, LANES), tab_u, jnp.asarray(_chunk_sum_matrix(), BF16), gate)


def _peer_v(off, w, x1, tab_v):
    t = off.shape[0]
    offs, pairs, rows, table, scratch = _peer_specs(t // PEER_STEP)
    flat = pl.BlockSpec((PEER_STEP, D_MODEL), lambda i: (i, 0))
    return pl.pallas_call(
        _peer_v_kernel,
        out_shape=jax.ShapeDtypeStruct((t, D_MODEL), F32),
        grid=(t // PEER_STEP,),
        in_specs=offs + [pairs, flat, table, _resident((N_PAIR, D_MODEL))],
        out_specs=flat,
        scratch_shapes=scratch,
        compiler_params=_params("arbitrary"),
        name="peer_v",
    )(off, off, w, x1, tab_v, jnp.asarray(_chunk_sum_matrix().T, BF16))


def kernel(x, ln_mix_g, w_in, conv_w, conv_b, w_gate_a, b_gate_a, w_gate_x, b_gate_x, lru_L, q_norm_g,
           k_norm_g, sinks, lru_out_g, attn_out_g, w_out, ln_ffn_g, w_query, sub_keys, expert_u, expert_v,
           rel_bias):
    batch, seq, _ = x.shape
    bias = _band_bias(rel_bias)
    x2 = x.reshape(batch * seq, D_MODEL)
    for l in range(w_in.shape[0]):
        xb, gb, q, k, v = _inproj(x2, ln_mix_g[l], w_in[l])
        y_lru = _rglru(xb, gb, conv_w[l], conv_b[l], w_gate_a[l], b_gate_a[l], w_gate_x[l], b_gate_x[l],
                       lru_L[l], lru_out_g[l], batch, seq)
        y_att = _swa(q, k, v, bias, q_norm_g[l], k_norm_g[l], sinks[l], attn_out_g[l], batch, seq)
        x1, xn = _outproj(x2, y_lru, y_att, w_out[l], ln_ffn_g[l])
        off, gate = _peer_topk(xn, w_query[l], sub_keys[l])
        w = _peer_u(off, gate, xn, _pack_table(expert_u[l]))
        x2 = _peer_v(off, w, x1, _pack_table(expert_v[l]))
    return x2.reshape(batch, seq, D_MODEL)
```

```python
import math

import jax
import jax.numpy as jnp
import numpy as np
from jax import lax
from jax.experimental import pallas as pl
from jax.experimental.pallas import tpu as pltpu

D_MODEL = 1024
LRU_WIDTH = 512
LRU_BLOCKS = 8
LRU_BLOCK = LRU_WIDTH // LRU_BLOCKS
CONV_WIDTH = 4
LRU_C = 8.0
N_HEADS = 8
N_KV_HEADS = 2
GROUP = N_HEADS // N_KV_HEADS
HEAD_DIM = 64
ATTN_WIDTH = N_HEADS * HEAD_DIM
KV_WIDTH = N_KV_HEADS * HEAD_DIM
WINDOW = 128
BLOCK = 128
N_BUCKETS = 32
MAX_DISTANCE = 128
PEER_HEADS = 8
N_KEYS = 128
N_EXPERTS = N_KEYS * N_KEYS
D_QUERY = 256
D_HALF = D_QUERY // 2
TOPK = 16
N_PAIR = PEER_HEADS * TOPK
IN_COLS = 2 * LRU_WIDTH + ATTN_WIDTH + 2 * KV_WIDTH
EPS = 1e-6
NEG_INF = -1e30
SCALE = HEAD_DIM ** -0.5

F32 = jnp.float32
BF16 = jnp.bfloat16
PACKED = jnp.uint32
LANES = 128
SUBLANES = 8
N_CHUNK = D_MODEL // LANES
WORD_ROWS = N_CHUNK // 2
MXU_ROWS = 256
TILE_PAIRS = MXU_ROWS // N_CHUNK
N_TILES = N_PAIR // TILE_PAIRS
VMEM_LIMIT = 52 * 1024 * 1024

ROW_TILE = 1024
PACK_TILE = 2048
LRU_TILE = 512
SWA_BLOCKS = 2
TOPK_TILE = 512
PEER_HALF = 64
PEER_STEP = 2 * PEER_HALF
PEER_GROUP = 256
STEPS_PER_GROUP = PEER_GROUP // PEER_STEP


def _rms(x, g):
    return x * lax.rsqrt(jnp.mean(x * x, axis=-1, keepdims=True) + EPS) * g


def _gelu(x):
    return 0.5 * x * (1.0 + jnp.tanh(math.sqrt(2.0 / math.pi) * (x + 0.044715 * (x * x * x))))


def _params(*sem):
    return pltpu.CompilerParams(dimension_semantics=sem, vmem_limit_bytes=VMEM_LIMIT)


def _resident(shape):
    zeros = (0,) * len(shape)
    return pl.BlockSpec(shape, lambda *_: zeros, pipeline_mode=pl.Buffered(1))


def _inproj_kernel(x_ref, g_ref, w_ref, xb_ref, gb_ref, q_ref, k_ref, v_ref):
    h = _rms(x_ref[...], g_ref[...])
    p = jnp.dot(h.astype(BF16), w_ref[...], preferred_element_type=F32)
    c0, c1, c2, c3 = LRU_WIDTH, 2 * LRU_WIDTH, 2 * LRU_WIDTH + ATTN_WIDTH, IN_COLS - KV_WIDTH
    xb_ref[...] = p[:, :c0]
    gb_ref[...] = p[:, c0:c1]
    q_ref[...] = p[:, c1:c2]
    k_ref[...] = p[:, c2:c3]
    v_ref[...] = p[:, c3:]


def _inproj(x2, g, w_in):
    t = x2.shape[0]
    widths = (LRU_WIDTH, LRU_WIDTH, ATTN_WIDTH, KV_WIDTH, KV_WIDTH)
    row = lambda w: pl.BlockSpec((ROW_TILE, w), lambda i: (i, 0))
    return pl.pallas_call(
        _inproj_kernel,
        out_shape=[jax.ShapeDtypeStruct((t, w), F32) for w in widths],
        grid=(t // ROW_TILE,),
        in_specs=[row(D_MODEL), _resident((1, D_MODEL)), _resident((D_MODEL, IN_COLS))],
        out_specs=[row(w) for w in widths],
        compiler_params=_params("parallel"),
        name="inproj",
    )(x2, g.reshape(1, D_MODEL), w_in.astype(BF16))


def _rglru_kernel(xb_ref, gb_ref, cw_ref, cb_ref, wg_ref, bg_ref, l_ref, og_ref, o_ref,
                  xs_scr, a_scr, b_scr, h_scr):
    tt, c = xb_ref.shape

    @pl.when(pl.program_id(1) == 0)
    def _():
        xs_scr[0:SUBLANES, :] = jnp.zeros((SUBLANES, c), F32)
        h_scr[...] = jnp.zeros_like(h_scr)

    xb = xb_ref[...]
    xs_scr[SUBLANES:SUBLANES + tt, :] = xb
    xc = cb_ref[...] + xb * cw_ref[CONV_WIDTH - 1:CONV_WIDTH, :]
    for back in range(1, CONV_WIDTH):
        tap = CONV_WIDTH - 1 - back
        xc = xc + xs_scr[SUBLANES - back:SUBLANES - back + tt, :] * cw_ref[tap:tap + 1, :]
    xs_scr[0:SUBLANES, :] = xb[tt - SUBLANES:, :]

    gates = jnp.dot(xc.astype(BF16), wg_ref[...], preferred_element_type=F32) + bg_ref[...]
    r = jax.nn.sigmoid(gates[:, :c])
    ig = jax.nn.sigmoid(gates[:, c:])
    lam = l_ref[...]
    softplus_neg = jnp.maximum(-lam, 0.0) + jnp.log1p(jnp.exp(-jnp.abs(lam)))
    log_a = (-LRU_C) * r * softplus_neg
    a = jnp.exp(log_a)
    b = jnp.sqrt(-jnp.tanh(log_a) * (a * a + 1.0)) * (ig * xc)
    a_scr[...] = a
    b_scr[...] = b

    rows = lax.broadcasted_iota(jnp.int32, (SUBLANES, c), 0)

    def tile(i, h):
        off = pl.multiple_of(i * SUBLANES, SUBLANES)
        at = a_scr[pl.ds(off, SUBLANES), :]
        bt = b_scr[pl.ds(off, SUBLANES), :]
        for d in (1, 2, 4):
            keep = rows >= d
            bt = jnp.where(keep, at * pltpu.roll(bt, d, axis=0) + bt, bt)
            at = jnp.where(keep, at * pltpu.roll(at, d, axis=0), at)
        ht = at * h + bt
        b_scr[pl.ds(off, SUBLANES), :] = ht
        return ht[SUBLANES - 1:SUBLANES, :]

    h_scr[...] = lax.fori_loop(0, tt // SUBLANES, tile, h_scr[...])
    y = b_scr[...] * _gelu(gb_ref[...])
    o_ref[...] = _rms(y, og_ref[...]).astype(o_ref.dtype)


def _rglru(xb, gb, conv_w, conv_b, w_gate_a, b_gate_a, w_gate_x, b_gate_x, lru_l, out_g, batch, seq):
    c = LRU_WIDTH
    eye = jnp.eye(LRU_BLOCKS, dtype=F32)
    dense = lambda w: jnp.einsum("nij,nm->nimj", w, eye).reshape(c, c)
    wg = jnp.concatenate([dense(w_gate_a), dense(w_gate_x)], axis=1).astype(BF16)
    bg = jnp.concatenate([b_gate_a.reshape(1, c), b_gate_x.reshape(1, c)], axis=1)
    blk = pl.BlockSpec((None, LRU_TILE, c), lambda b, j: (b, j, 0))
    out = pl.pallas_call(
        _rglru_kernel,
        out_shape=jax.ShapeDtypeStruct((batch, seq, c), BF16),
        grid=(batch, seq // LRU_TILE),
        in_specs=[blk, blk, _resident((CONV_WIDTH, c)), _resident((1, c)), _resident((c, 2 * c)),
                  _resident((1, 2 * c)), _resident((1, c)), _resident((1, c))],
        out_specs=blk,
        scratch_shapes=[pltpu.VMEM((LRU_TILE + SUBLANES, c), F32), pltpu.VMEM((LRU_TILE, c), F32),
                        pltpu.VMEM((LRU_TILE, c), F32), pltpu.VMEM((1, c), F32)],
        compiler_params=_params("parallel", "arbitrary"),
        name="rglru",
    )(xb.reshape(batch, seq, c), gb.reshape(batch, seq, c), conv_w, conv_b.reshape(1, c), wg, bg,
      lru_l.reshape(1, c), out_g.reshape(1, c))
    return out.reshape(batch * seq, c)


def _bias_kernel(rb_ref, onehot_ref, o_ref):
    o_ref[...] = jnp.dot(rb_ref[...], onehot_ref[...], preferred_element_type=F32,
                         precision=lax.Precision.HIGHEST)


def _t5_bucket(rel):
    n = jnp.maximum(rel, 0)
    max_exact = N_BUCKETS // 2
    nf = jnp.maximum(n, 1).astype(F32)
    large = max_exact + jnp.floor(jnp.log(nf / max_exact) / math.log(MAX_DISTANCE / max_exact)
                                  * (N_BUCKETS - max_exact)).astype(jnp.int32)
    large = jnp.minimum(large, N_BUCKETS - 1)
    return jnp.where(n < max_exact, n, large)


def _band_bias(rel_bias):
    i = jnp.arange(BLOCK)[:, None]
    j = jnp.arange(2 * BLOCK)[None, :]
    bucket = _t5_bucket(BLOCK + i - j).reshape(1, -1)
    onehot = (bucket == jnp.arange(N_BUCKETS)[:, None]).astype(F32)
    out = pl.pallas_call(
        _bias_kernel,
        out_shape=jax.ShapeDtypeStruct((N_HEADS, BLOCK * 2 * BLOCK), F32),
        name="band_bias",
    )(rel_bias.astype(F32).T, onehot)
    return out.reshape(N_HEADS, BLOCK, 2 * BLOCK)


def _swa_kernel(sink_ref, q_ref, kc_ref, kp_ref, vc_ref, vp_ref, bias_ref, qg_ref, kg_ref, og_ref, o_ref):
    kk = jnp.concatenate([kp_ref[...], kc_ref[...]], axis=0)
    vv = jnp.concatenate([vp_ref[...], vc_ref[...]], axis=0).astype(BF16)
    k_n = [_rms(kk[:, hk * HEAD_DIM:(hk + 1) * HEAD_DIM], kg_ref[...]).astype(BF16) for hk in range(N_KV_HEADS)]
    shape = (GROUP * BLOCK, 2 * BLOCK)
    qi = lax.broadcasted_iota(jnp.int32, shape, 0) % BLOCK
    kj = lax.broadcasted_iota(jnp.int32, shape, 1)
    for sub in range(SWA_BLOCKS):
        first_key = jnp.where(pl.program_id(1) > 0, 0, BLOCK) if sub == 0 else 0
        valid = (kj > jnp.maximum(qi + (BLOCK - WINDOW), first_key - 1)) & (kj <= qi + BLOCK)
        rows = slice(sub * BLOCK, (sub + 1) * BLOCK)
        keys = slice(sub * BLOCK, (sub + 2) * BLOCK)
        outs = []
        for hk in range(N_KV_HEADS):
            qs, sinks = [], []
            for g in range(GROUP):
                h = hk * GROUP + g
                qs.append(_rms(q_ref[rows, h * HEAD_DIM:(h + 1) * HEAD_DIM], qg_ref[...]).astype(BF16))
                sinks.append(jnp.full((BLOCK, 1), sink_ref[h], F32))
            q_n = jnp.concatenate(qs, axis=0)
            sink = jnp.concatenate(sinks, axis=0)
            s = lax.dot_general(q_n, k_n[hk][keys], (((1,), (1,)), ((), ())), preferred_element_type=F32)
            s = s * SCALE + bias_ref[hk * GROUP:(hk + 1) * GROUP].reshape(shape)
            s = jnp.where(valid, s, NEG_INF)
            m = jnp.maximum(jnp.max(s, axis=-1, keepdims=True), sink)
            p = jnp.exp(s - m)
            denom = jnp.sum(p, axis=-1, keepdims=True) + jnp.exp(sink - m)
            o = jnp.dot(p.astype(BF16), vv[keys, hk * HEAD_DIM:(hk + 1) * HEAD_DIM],
                        preferred_element_type=F32) / denom
            outs.extend(o[g * BLOCK:(g + 1) * BLOCK] for g in range(GROUP))
        y = jnp.concatenate(outs, axis=-1)
        o_ref[rows, :] = _rms(y, og_ref[...]).astype(o_ref.dtype)


def _swa(q, k, v, bias, q_norm_g, k_norm_g, sinks, out_g, batch, seq):
    nb = seq // BLOCK
    cur = lambda w: pl.BlockSpec((None, SWA_BLOCKS * BLOCK, w), lambda b, n: (b, n, 0))
    prev = lambda w: pl.BlockSpec((None, BLOCK, w), lambda b, n: (b, jnp.maximum(SWA_BLOCKS * n - 1, 0), 0))
    k3 = k.reshape(batch, seq, KV_WIDTH)
    v3 = v.reshape(batch, seq, KV_WIDTH)
    out = pl.pallas_call(
        _swa_kernel,
        out_shape=jax.ShapeDtypeStruct((batch, seq, ATTN_WIDTH), BF16),
        grid=(batch, nb // SWA_BLOCKS),
        in_specs=[pl.BlockSpec(memory_space=pltpu.SMEM), cur(ATTN_WIDTH), cur(KV_WIDTH), prev(KV_WIDTH),
                  cur(KV_WIDTH), prev(KV_WIDTH), _resident((N_HEADS, BLOCK, 2 * BLOCK)),
                  _resident((1, HEAD_DIM)), _resident((1, HEAD_DIM)), _resident((1, ATTN_WIDTH))],
        out_specs=cur(ATTN_WIDTH),
        compiler_params=_params("parallel", "parallel"),
        name="swa",
    )(sinks.astype(F32), q.reshape(batch, seq, ATTN_WIDTH), k3, k3, v3, v3, bias,
      q_norm_g.reshape(1, HEAD_DIM), k_norm_g.reshape(1, HEAD_DIM), out_g.reshape(1, ATTN_WIDTH))
    return out.reshape(batch * seq, ATTN_WIDTH)


def _outproj_kernel(x_ref, ml_ref, ma_ref, w_ref, g_ref, x1_ref, xn_ref):
    acc = jnp.dot(ml_ref[...], w_ref[:LRU_WIDTH, :], preferred_element_type=F32)
    acc = acc + jnp.dot(ma_ref[...], w_ref[LRU_WIDTH:, :], preferred_element_type=F32)
    x1 = x_ref[...] + acc
    x1_ref[...] = x1
    xn_ref[...] = _rms(x1, g_ref[...])


def _outproj(x2, y_lru, y_att, w_out, g):
    t = x2.shape[0]
    row = lambda w: pl.BlockSpec((ROW_TILE, w), lambda i: (i, 0))
    return pl.pallas_call(
        _outproj_kernel,
        out_shape=[jax.ShapeDtypeStruct((t, D_MODEL), F32)] * 2,
        grid=(t // ROW_TILE,),
        in_specs=[row(D_MODEL), row(LRU_WIDTH), row(ATTN_WIDTH), _resident((D_MODEL, D_MODEL)),
                  _resident((1, D_MODEL))],
        out_specs=[row(D_MODEL)] * 2,
        compiler_params=_params("parallel"),
        name="outproj",
    )(x2, y_lru, y_att, w_out.astype(BF16), g.reshape(1, D_MODEL))


def _pair_candidates(s0, i0, s1, i1):
    b_ids = lax.broadcasted_iota(jnp.int32, (SUBLANES,) + s1.shape[1:], 0)
    vals, ids = [], []
    a = 0
    while TOPK // (a + 1) > 1:
        n_b = TOPK // (a + 1)
        n_rows = -(-n_b // SUBLANES) * SUBLANES
        v = s0[a:a + 1] + s1[:n_rows]
        if n_b < n_rows:
            v = jnp.where(b_ids < n_b, v, -jnp.inf)
        vals.append(v)
        ids.append(i0[a:a + 1] * N_KEYS + i1[:n_rows])
        a += 1
    vals.append(s0[a:] + s1[0:1])
    ids.append(i0[a:] * N_KEYS + i1[0:1])
    return jnp.concatenate(vals, axis=0), jnp.concatenate(ids, axis=0)


def _top_rows(s, payload, k):
    n = s.shape[0]
    rows = lax.broadcasted_iota(jnp.int32, s.shape, 0).astype(F32)
    vals, picks = [], []
    for _ in range(k):
        m = jnp.max(s, axis=0, keepdims=True)
        first = jnp.min(jnp.where(s == m, rows, float(n)), axis=0, keepdims=True)
        hit = rows == first
        vals.append(m)
        if payload is None:
            picks.append(first)
        else:
            picks.append(jnp.max(jnp.where(hit, payload, -1.0), axis=0, keepdims=True))
        s = jnp.where(hit, -jnp.inf, s)
    return jnp.concatenate(vals, axis=0), jnp.concatenate(picks, axis=0)


def _peer_topk_kernel(xn_ref, wq_ref, keys_ref, off_ref, g_ref, q_scr, off_scr, g_scr):
    q_t = lax.dot_general(wq_ref[...], xn_ref[...].astype(BF16), (((1,), (1,)), ((), ())),
                          preferred_element_type=F32)
    q_scr[...] = q_t.astype(BF16)

    def head(h, carry):
        tops = []
        for half in range(2):
            hc = h * 2 + half
            q_hc = q_scr[pl.ds(pl.multiple_of(hc * D_HALF, D_HALF), D_HALF), :]
            s = jnp.dot(keys_ref[hc], q_hc, preferred_element_type=F32)
            tops.append(_top_rows(s, None, TOPK))
        (s0, i0), (s1, i1) = tops
        best, idx = _top_rows(*_pair_candidates(s0, i0, s1, i1), TOPK)
        e = jnp.exp(best - best[0:1])
        rows = pl.ds(pl.multiple_of(h * TOPK, TOPK), TOPK)
        off_scr[rows, :] = (idx * WORD_ROWS).astype(jnp.int32)
        g_scr[rows, :] = e / jnp.sum(e, axis=0, keepdims=True)
        return carry

    lax.fori_loop(0, PEER_HEADS, head, 0)
    off_ref[...] = off_scr[...].T
    g_ref[...] = g_scr[...].T


def _peer_topk(xn, w_query, sub_keys):
    t = xn.shape[0]
    nq = PEER_HEADS * D_QUERY
    out_blk = pl.BlockSpec((TOPK_TILE, N_PAIR), lambda i: (i, 0))
    return pl.pallas_call(
        _peer_topk_kernel,
        out_shape=[jax.ShapeDtypeStruct((t, N_PAIR), jnp.int32), jax.ShapeDtypeStruct((t, N_PAIR), F32)],
        grid=(t // TOPK_TILE,),
        in_specs=[pl.BlockSpec((TOPK_TILE, D_MODEL), lambda i: (i, 0)), _resident((nq, D_MODEL)),
                  _resident((2 * PEER_HEADS, N_KEYS, D_HALF))],
        out_specs=[out_blk, out_blk],
        scratch_shapes=[pltpu.VMEM((nq, TOPK_TILE), BF16), pltpu.VMEM((N_PAIR, TOPK_TILE), jnp.int32),
                        pltpu.VMEM((N_PAIR, TOPK_TILE), F32)],
        compiler_params=_params("parallel"),
        name="peer_topk",
    )(xn, w_query.T.astype(BF16), sub_keys.reshape(2 * PEER_HEADS, N_KEYS, D_HALF).astype(BF16))


def _pack_kernel(x_ref, o_ref):
    rows = x_ref.shape[0]
    for j in range(WORD_ROWS):
        lo = x_ref[:, j * LANES:(j + 1) * LANES]
        hi = x_ref[:, (WORD_ROWS + j) * LANES:(WORD_ROWS + j + 1) * LANES]
        o_ref[pl.ds(j, rows, stride=WORD_ROWS), :] = pltpu.pack_elementwise([lo, hi], packed_dtype=BF16)


def _pack_table(tab):
    n = tab.shape[0]
    return pl.pallas_call(
        _pack_kernel,
        out_shape=jax.ShapeDtypeStruct((n * WORD_ROWS, LANES), PACKED),
        grid=(n // PACK_TILE,),
        in_specs=[pl.BlockSpec((PACK_TILE, D_MODEL), lambda i: (i, 0))],
        out_specs=pl.BlockSpec((PACK_TILE * WORD_ROWS, LANES), lambda i: (i, 0)),
        compiler_params=_params("parallel"),
        name="pack_table",
    )(tab)


def _slot_chunk(slot):
    return slot // 2 + WORD_ROWS * (slot % 2)


def _chunk_sum_matrix():
    return (np.arange(D_MODEL)[:, None] // N_CHUNK == np.arange(N_PAIR)[None, :]).astype(np.float32)


def _gathered_tile(off_ref, tab_ref, t, j):
    parts = [tab_ref[pl.ds(pl.multiple_of(off_ref[t, TILE_PAIRS * j + i], WORD_ROWS), WORD_ROWS), :]
             for i in range(TILE_PAIRS)]
    return pltpu.bitcast(jnp.concatenate(parts, axis=0), BF16)


def _select_dot(a, sel):
    out = None
    for _ in range(3):
        piece = a.astype(BF16)
        part = jnp.dot(piece, sel, preferred_element_type=F32)
        out = part if out is None else out + part
        a = a - piece.astype(F32)
    return out


def _swap_tile_rows(tiles):
    tiles = list(tiles)
    rows = lax.broadcasted_iota(jnp.int32, tiles[0].shape, 0)
    d = SUBLANES // 2
    while d:
        low = (rows & d) == 0
        out = list(tiles)
        for i in range(SUBLANES):
            if not i & d:
                a, b = tiles[i], tiles[i + d]
                out[i] = jnp.where(low, a, pltpu.roll(b, d, axis=0))
                out[i + d] = jnp.where(low, pltpu.roll(a, SUBLANES - d, axis=0), b)
        tiles = out
        d //= 2
    return tiles


def _split_bf16(a):
    hi = a.astype(BF16)
    return jnp.concatenate([hi, (a - hi.astype(F32)).astype(BF16)], axis=0)


def _for_each_token(off_now, off_next, off_bufs, sem, compute):
    s = pl.program_id(0)

    def fetch(src, which):
        return pltpu.make_async_copy(src.at[pl.ds(which * PEER_HALF, PEER_HALF)], off_bufs[which], sem.at[which])

    def consume(which):
        for t in range(PEER_HALF):
            compute(which * PEER_HALF + t, off_bufs[which], t)

    not_last = s + 1 < pl.num_programs(0)

    @pl.when(s == 0)
    def _():
        first = fetch(off_now, 0)
        first.start()
        first.wait()

    fetch(off_now, 1).start()
    consume(0)
    fetch(off_now, 1).wait()

    @pl.when(not_last)
    def _():
        fetch(off_next, 0).start()

    consume(1)

    @pl.when(not_last)
    def _():
        fetch(off_next, 0).wait()


def _group_row0():
    return pl.multiple_of((pl.program_id(0) % STEPS_PER_GROUP) * PEER_STEP, PEER_STEP)


def _peer_u_kernel(off_now, off_next, x_ref, tab_ref, sum_ref, gate_ref, w_ref, off_a, off_b, sem, z_scr):
    shape = (2 * N_CHUNK, MXU_ROWS)
    own = (lax.broadcasted_iota(jnp.int32, shape, 0) % N_CHUNK) == _slot_chunk(lax.broadcasted_iota(jnp.int32, shape, 1) % N_CHUNK)
    row0 = _group_row0()

    def compute(tok, off_ref, t):
        lhs = _split_bf16(x_ref[tok])
        zs = []
        for j in range(N_TILES):
            o = lax.dot_general(lhs, _gathered_tile(off_ref, tab_ref, t, j), (((1,), (1,)), ((), ())),
                                preferred_element_type=F32)
            zs.append(jnp.sum(jnp.where(own, o, 0.0), axis=0, keepdims=True))
        z_scr[pl.ds(row0 + tok, 1), :] = jnp.concatenate(zs, axis=1)

    _for_each_token(off_now, off_next, (off_a, off_b), sem, compute)

    @pl.when(pl.program_id(0) % STEPS_PER_GROUP == STEPS_PER_GROUP - 1)
    def _():
        act = _select_dot(z_scr[...], sum_ref[...])
        w_ref[...] = gate_ref[...] * _gelu(act)


def _peer_v_kernel(off_now, off_next, w_ref, x1_ref, tab_ref, spread_ref, y_ref, off_a, off_b, sem, wx_scr):
    shape = (N_CHUNK, D_MODEL)
    own = lax.broadcasted_iota(jnp.int32, shape, 0) == _slot_chunk(lax.broadcasted_iota(jnp.int32, shape, 1) % N_CHUNK)
    row0 = _group_row0()

    @pl.when(pl.program_id(0) % STEPS_PER_GROUP == 0)
    def _():
        wx_scr[...] = _select_dot(w_ref[...], spread_ref[...])

    def compute(tok, off_ref, t):
        lhs = _split_bf16(jnp.where(own, jnp.broadcast_to(wx_scr[pl.ds(row0 + tok, 1), :], shape), 0.0))
        r = None
        for j in range(N_TILES):
            part = jnp.dot(lhs[:, j * MXU_ROWS:(j + 1) * MXU_ROWS], _gathered_tile(off_ref, tab_ref, t, j),
                           preferred_element_type=F32)
            r = part if r is None else r + part
        pending.append(r[:N_CHUNK] + r[N_CHUNK:])
        if len(pending) == SUBLANES:
            by_chunk = _swap_tile_rows(pending)
            pending.clear()
            toks = slice(tok + 1 - SUBLANES, tok + 1)
            for c in range(N_CHUNK):
                cols = slice(c * LANES, (c + 1) * LANES)
                y_ref[toks, cols] = x1_ref[toks, cols] + by_chunk[c]

    pending = []
    _for_each_token(off_now, off_next, (off_a, off_b), sem, compute)


def _peer_specs(n_steps):
    pairs = pl.BlockSpec((PEER_GROUP, N_PAIR), lambda i: (i // STEPS_PER_GROUP, 0))
    rows = pl.BlockSpec((PEER_STEP, N_CHUNK, LANES), lambda i: (i, 0, 0))
    table = _resident((N_EXPERTS * WORD_ROWS, LANES))
    scratch = ([pltpu.SMEM((PEER_HALF, N_PAIR), jnp.int32)] * 2 + [pltpu.SemaphoreType.DMA((2,))]
               + [pltpu.VMEM((PEER_GROUP, D_MODEL), F32)])
    offs = [pl.BlockSpec((PEER_STEP, N_PAIR), lambda i: (i, 0)),
            pl.BlockSpec((PEER_STEP, N_PAIR), lambda i: (jnp.minimum(i + 1, n_steps - 1), 0))]
    return offs, pairs, rows, table, scratch


def _peer_u(off, gate, xn, tab_u):
    t = off.shape[0]
    offs, pairs, rows, table, scratch = _peer_specs(t // PEER_STEP)
    return pl.pallas_call(
        _peer_u_kernel,
        out_shape=jax.ShapeDtypeStruct((t, N_PAIR), F32),
        grid=(t // PEER_STEP,),
        in_specs=offs + [rows, table, _resident((D_MODEL, N_PAIR)), pairs],
        out_specs=pairs,
        scratch_shapes=scratch,
        compiler_params=_params("arbitrary"),
        name="peer_u",
    )(off, off, xn.reshape(t, N_CHUNK, LANES), tab_u, jnp.asarray(_chunk_sum_matrix(), BF16), gate)


def _peer_v(off, w, x1, tab_v):
    t = off.shape[0]
    offs, pairs, rows, table, scratch = _peer_specs(t // PEER_STEP)
    flat = pl.BlockSpec((PEER_STEP, D_MODEL), lambda i: (i, 0))
    return pl.pallas_call(
        _peer_v_kernel,
        out_shape=jax.ShapeDtypeStruct((t, D_MODEL), F32),
        grid=(t // PEER_STEP,),
        in_specs=offs + [pairs, flat, table, _resident((N_PAIR, D_MODEL))],
        out_specs=flat,
        scratch_shapes=scratch,
        compiler_params=_params("arbitrary"),
        name="peer_v",
    )(off, off, w, x1, tab_v, jnp.asarray(_chunk_sum_matrix().T, BF16))


def kernel(x, ln_mix_g, w_in, conv_w, conv_b, w_gate_a, b_gate_a, w_gate_x, b_gate_x, lru_L, q_norm_g,
           k_norm_g, sinks, lru_out_g, attn_out_g, w_out, ln_ffn_g, w_query, sub_keys, expert_u, expert_v,
           rel_bias):
    batch, seq, _ = x.shape
    bias = _band_bias(rel_bias)
    x2 = x.reshape(batch * seq, D_MODEL)
    for l in range(w_in.shape[0]):
        xb, gb, q, k, v = _inproj(x2, ln_mix_g[l], w_in[l])
        y_lru = _rglru(xb, gb, conv_w[l], conv_b[l], w_gate_a[l], b_gate_a[l], w_gate_x[l], b_gate_x[l],
                       lru_L[l], lru_out_g[l], batch, seq)
        y_att = _swa(q, k, v, bias, q_norm_g[l], k_norm_g[l], sinks[l], attn_out_g[l], batch, seq)
        x1, xn = _outproj(x2, y_lru, y_att, w_out[l], ln_ffn_g[l])
        off, gate = _peer_topk(xn, w_query[l], sub_keys[l])
        w = _peer_u(off, gate, xn, _pack_table(expert_u[l]))
        x2 = _peer_v(off, w, x1, _pack_table(expert_v[l]))
    return x2.reshape(batch, seq, D_MODEL)
```

```python
import math

import jax
import jax.numpy as jnp
import numpy as np
from jax import lax
from jax.experimental import pallas as pl
from jax.experimental.pallas import tpu as pltpu

D_MODEL = 1024
LRU_WIDTH = 512
LRU_BLOCKS = 8
LRU_BLOCK = LRU_WIDTH // LRU_BLOCKS
CONV_WIDTH = 4
LRU_C = 8.0
N_HEADS = 8
N_KV_HEADS = 2
GROUP = N_HEADS // N_KV_HEADS
HEAD_DIM = 64
ATTN_WIDTH = N_HEADS * HEAD_DIM
KV_WIDTH = N_KV_HEADS * HEAD_DIM
WINDOW = 128
BLOCK = 128
N_BUCKETS = 32
MAX_DISTANCE = 128
PEER_HEADS = 8
N_KEYS = 128
N_EXPERTS = N_KEYS * N_KEYS
D_QUERY = 256
D_HALF = D_QUERY // 2
TOPK = 16
N_PAIR = PEER_HEADS * TOPK
IN_COLS = 2 * LRU_WIDTH + ATTN_WIDTH + 2 * KV_WIDTH
EPS = 1e-6
NEG_INF = -1e30
SCALE = HEAD_DIM ** -0.5

F32 = jnp.float32
BF16 = jnp.bfloat16
PACKED = jnp.uint32
LANES = 128
SUBLANES = 8
N_CHUNK = D_MODEL // LANES
WORD_ROWS = N_CHUNK // 2
MXU_ROWS = 256
TILE_PAIRS = MXU_ROWS // N_CHUNK
N_TILES = N_PAIR // TILE_PAIRS
VMEM_LIMIT = 52 * 1024 * 1024

ROW_TILE = 1024
PACK_TILE = 2048
LRU_TILE = 512
SWA_BLOCKS = 4
TOPK_TILE = 512
PEER_HALF = 64
PEER_STEP = 2 * PEER_HALF
PEER_GROUP = 256
STEPS_PER_GROUP = PEER_GROUP // PEER_STEP


def _rms(x, g):
    return x * lax.rsqrt(jnp.mean(x * x, axis=-1, keepdims=True) + EPS) * g


def _gelu(x):
    return 0.5 * x * (1.0 + jnp.tanh(math.sqrt(2.0 / math.pi) * (x + 0.044715 * (x * x * x))))


def _params(*sem):
    return pltpu.CompilerParams(dimension_semantics=sem, vmem_limit_bytes=VMEM_LIMIT)


def _resident(shape):
    zeros = (0,) * len(shape)
    return pl.BlockSpec(shape, lambda *_: zeros, pipeline_mode=pl.Buffered(1))


def _inproj_kernel(x_ref, g_ref, w_ref, xb_ref, gb_ref, q_ref, k_ref, v_ref):
    h = _rms(x_ref[...], g_ref[...])
    p = jnp.dot(h.astype(BF16), w_ref[...], preferred_element_type=F32)
    c0, c1, c2, c3 = LRU_WIDTH, 2 * LRU_WIDTH, 2 * LRU_WIDTH + ATTN_WIDTH, IN_COLS - KV_WIDTH
    xb_ref[...] = p[:, :c0]
    gb_ref[...] = p[:, c0:c1]
    q_ref[...] = p[:, c1:c2]
    k_ref[...] = p[:, c2:c3]
    v_ref[...] = p[:, c3:]


def _inproj(x2, g, w_in):
    t = x2.shape[0]
    widths = (LRU_WIDTH, LRU_WIDTH, ATTN_WIDTH, KV_WIDTH, KV_WIDTH)
    row = lambda w: pl.BlockSpec((ROW_TILE, w), lambda i: (i, 0))
    return pl.pallas_call(
        _inproj_kernel,
        out_shape=[jax.ShapeDtypeStruct((t, w), F32) for w in widths],
        grid=(t // ROW_TILE,),
        in_specs=[row(D_MODEL), _resident((1, D_MODEL)), _resident((D_MODEL, IN_COLS))],
        out_specs=[row(w) for w in widths],
        compiler_params=_params("parallel"),
        name="inproj",
    )(x2, g.reshape(1, D_MODEL), w_in.astype(BF16))


def _rglru_kernel(xb_ref, gb_ref, cw_ref, cb_ref, wg_ref, bg_ref, l_ref, og_ref, o_ref,
                  xs_scr, a_scr, b_scr, h_scr):
    tt, c = xb_ref.shape

    @pl.when(pl.program_id(1) == 0)
    def _():
        xs_scr[0:SUBLANES, :] = jnp.zeros((SUBLANES, c), F32)
        h_scr[...] = jnp.zeros_like(h_scr)

    xb = xb_ref[...]
    xs_scr[SUBLANES:SUBLANES + tt, :] = xb
    xc = cb_ref[...] + xb * cw_ref[CONV_WIDTH - 1:CONV_WIDTH, :]
    for back in range(1, CONV_WIDTH):
        tap = CONV_WIDTH - 1 - back
        xc = xc + xs_scr[SUBLANES - back:SUBLANES - back + tt, :] * cw_ref[tap:tap + 1, :]
    xs_scr[0:SUBLANES, :] = xb[tt - SUBLANES:, :]

    gates = jnp.dot(xc.astype(BF16), wg_ref[...], preferred_element_type=F32) + bg_ref[...]
    r = jax.nn.sigmoid(gates[:, :c])
    ig = jax.nn.sigmoid(gates[:, c:])
    lam = l_ref[...]
    softplus_neg = jnp.maximum(-lam, 0.0) + jnp.log1p(jnp.exp(-jnp.abs(lam)))
    log_a = (-LRU_C) * r * softplus_neg
    a = jnp.exp(log_a)
    b = jnp.sqrt(-jnp.tanh(log_a) * (a * a + 1.0)) * (ig * xc)
    a_scr[...] = a
    b_scr[...] = b

    rows = lax.broadcasted_iota(jnp.int32, (SUBLANES, c), 0)

    def tile(i, h):
        off = pl.multiple_of(i * SUBLANES, SUBLANES)
        at = a_scr[pl.ds(off, SUBLANES), :]
        bt = b_scr[pl.ds(off, SUBLANES), :]
        for d in (1, 2, 4):
            keep = rows >= d
            bt = jnp.where(keep, at * pltpu.roll(bt, d, axis=0) + bt, bt)
            at = jnp.where(keep, at * pltpu.roll(at, d, axis=0), at)
        ht = at * h + bt
        b_scr[pl.ds(off, SUBLANES), :] = ht
        return ht[SUBLANES - 1:SUBLANES, :]

    h_scr[...] = lax.fori_loop(0, tt // SUBLANES, tile, h_scr[...])
    y = b_scr[...] * _gelu(gb_ref[...])
    o_ref[...] = _rms(y, og_ref[...]).astype(o_ref.dtype)


def _rglru(xb, gb, conv_w, conv_b, w_gate_a, b_gate_a, w_gate_x, b_gate_x, lru_l, out_g, batch, seq):
    c = LRU_WIDTH
    eye = jnp.eye(LRU_BLOCKS, dtype=F32)
    dense = lambda w: jnp.einsum("nij,nm->nimj", w, eye).reshape(c, c)
    wg = jnp.concatenate([dense(w_gate_a), dense(w_gate_x)], axis=1).astype(BF16)
    bg = jnp.concatenate([b_gate_a.reshape(1, c), b_gate_x.reshape(1, c)], axis=1)
    blk = pl.BlockSpec((None, LRU_TILE, c), lambda b, j: (b, j, 0))
    out = pl.pallas_call(
        _rglru_kernel,
        out_shape=jax.ShapeDtypeStruct((batch, seq, c), BF16),
        grid=(batch, seq // LRU_TILE),
        in_specs=[blk, blk, _resident((CONV_WIDTH, c)), _resident((1, c)), _resident((c, 2 * c)),
                  _resident((1, 2 * c)), _resident((1, c)), _resident((1, c))],
        out_specs=blk,
        scratch_shapes=[pltpu.VMEM((LRU_TILE + SUBLANES, c), F32), pltpu.VMEM((LRU_TILE, c), F32),
                        pltpu.VMEM((LRU_TILE, c), F32), pltpu.VMEM((1, c), F32)],
        compiler_params=_params("parallel", "arbitrary"),
        name="rglru",
    )(xb.reshape(batch, seq, c), gb.reshape(batch, seq, c), conv_w, conv_b.reshape(1, c), wg, bg,
      lru_l.reshape(1, c), out_g.reshape(1, c))
    return out.reshape(batch * seq, c)


def _bias_kernel(rb_ref, onehot_ref, o_ref):
    o_ref[...] = jnp.dot(rb_ref[...], onehot_ref[...], preferred_element_type=F32,
                         precision=lax.Precision.HIGHEST)


def _t5_bucket(rel):
    n = jnp.maximum(rel, 0)
    max_exact = N_BUCKETS // 2
    nf = jnp.maximum(n, 1).astype(F32)
    large = max_exact + jnp.floor(jnp.log(nf / max_exact) / math.log(MAX_DISTANCE / max_exact)
                                  * (N_BUCKETS - max_exact)).astype(jnp.int32)
    large = jnp.minimum(large, N_BUCKETS - 1)
    return jnp.where(n < max_exact, n, large)


def _band_bias(rel_bias):
    i = jnp.arange(BLOCK)[:, None]
    j = jnp.arange(2 * BLOCK)[None, :]
    bucket = _t5_bucket(BLOCK + i - j).reshape(1, -1)
    onehot = (bucket == jnp.arange(N_BUCKETS)[:, None]).astype(F32)
    out = pl.pallas_call(
        _bias_kernel,
        out_shape=jax.ShapeDtypeStruct((N_HEADS, BLOCK * 2 * BLOCK), F32),
        name="band_bias",
    )(rel_bias.astype(F32).T, onehot)
    return out.reshape(N_HEADS, BLOCK, 2 * BLOCK)


def _swa_kernel(sink_ref, q_ref, kc_ref, kp_ref, vc_ref, vp_ref, bias_ref, qg_ref, kg_ref, og_ref, o_ref):
    kk = jnp.concatenate([kp_ref[...], kc_ref[...]], axis=0)
    vv = jnp.concatenate([vp_ref[...], vc_ref[...]], axis=0).astype(BF16)
    k_n = [_rms(kk[:, hk * HEAD_DIM:(hk + 1) * HEAD_DIM], kg_ref[...]).astype(BF16) for hk in range(N_KV_HEADS)]
    shape = (GROUP * BLOCK, 2 * BLOCK)
    qi = lax.broadcasted_iota(jnp.int32, shape, 0) % BLOCK
    kj = lax.broadcasted_iota(jnp.int32, shape, 1)
    chains = [(sub, hk) for sub in range(SWA_BLOCKS) for hk in range(N_KV_HEADS)]
    rows = lambda sub: slice(sub * BLOCK, (sub + 1) * BLOCK)
    keys = lambda sub: slice(sub * BLOCK, (sub + 2) * BLOCK)
    sink, s, m, p, denom, o = {}, {}, {}, {}, {}, {}
    for sub, hk in chains:
        first_key = jnp.where(pl.program_id(1) > 0, 0, BLOCK) if sub == 0 else 0
        valid = (kj > jnp.maximum(qi + (BLOCK - WINDOW), first_key - 1)) & (kj <= qi + BLOCK)
        heads = range(hk * GROUP, (hk + 1) * GROUP)
        q_n = jnp.concatenate([_rms(q_ref[rows(sub), h * HEAD_DIM:(h + 1) * HEAD_DIM], qg_ref[...]).astype(BF16)
                               for h in heads], axis=0)
        sink[sub, hk] = jnp.concatenate([jnp.full((BLOCK, 1), sink_ref[h], F32) for h in heads], axis=0)
        sc = lax.dot_general(q_n, k_n[hk][keys(sub)], (((1,), (1,)), ((), ())), preferred_element_type=F32)
        sc = sc * SCALE + bias_ref[hk * GROUP:(hk + 1) * GROUP].reshape(shape)
        s[sub, hk] = jnp.where(valid, sc, NEG_INF)
    for c in chains:
        m[c] = jnp.maximum(jnp.max(s[c], axis=-1, keepdims=True), sink[c])
    for c in chains:
        p[c] = jnp.exp(s[c] - m[c])
        denom[c] = jnp.sum(p[c], axis=-1, keepdims=True) + jnp.exp(sink[c] - m[c])
    for sub, hk in chains:
        o[sub, hk] = jnp.dot(p[sub, hk].astype(BF16), vv[keys(sub), hk * HEAD_DIM:(hk + 1) * HEAD_DIM],
                             preferred_element_type=F32) / denom[sub, hk]
    for sub in range(SWA_BLOCKS):
        y = jnp.concatenate([o[sub, hk][g * BLOCK:(g + 1) * BLOCK] for hk in range(N_KV_HEADS)
                             for g in range(GROUP)], axis=-1)
        o_ref[rows(sub), :] = _rms(y, og_ref[...]).astype(o_ref.dtype)


def _swa(q, k, v, bias, q_norm_g, k_norm_g, sinks, out_g, batch, seq):
    nb = seq // BLOCK
    cur = lambda w: pl.BlockSpec((None, SWA_BLOCKS * BLOCK, w), lambda b, n: (b, n, 0))
    prev = lambda w: pl.BlockSpec((None, BLOCK, w), lambda b, n: (b, jnp.maximum(SWA_BLOCKS * n - 1, 0), 0))
    k3 = k.reshape(batch, seq, KV_WIDTH)
    v3 = v.reshape(batch, seq, KV_WIDTH)
    out = pl.pallas_call(
        _swa_kernel,
        out_shape=jax.ShapeDtypeStruct((batch, seq, ATTN_WIDTH), BF16),
        grid=(batch, nb // SWA_BLOCKS),
        in_specs=[pl.BlockSpec(memory_space=pltpu.SMEM), cur(ATTN_WIDTH), cur(KV_WIDTH), prev(KV_WIDTH),
                  cur(KV_WIDTH), prev(KV_WIDTH), _resident((N_HEADS, BLOCK, 2 * BLOCK)),
                  _resident((1, HEAD_DIM)), _resident((1, HEAD_DIM)), _resident((1, ATTN_WIDTH))],
        out_specs=cur(ATTN_WIDTH),
        compiler_params=_params("parallel", "parallel"),
        name="swa",
    )(sinks.astype(F32), q.reshape(batch, seq, ATTN_WIDTH), k3, k3, v3, v3, bias,
      q_norm_g.reshape(1, HEAD_DIM), k_norm_g.reshape(1, HEAD_DIM), out_g.reshape(1, ATTN_WIDTH))
    return out.reshape(batch * seq, ATTN_WIDTH)


def _outproj_kernel(x_ref, ml_ref, ma_ref, w_ref, g_ref, x1_ref, xn_ref):
    acc = jnp.dot(ml_ref[...], w_ref[:LRU_WIDTH, :], preferred_element_type=F32)
    acc = acc + jnp.dot(ma_ref[...], w_ref[LRU_WIDTH:, :], preferred_element_type=F32)
    x1 = x_ref[...] + acc
    x1_ref[...] = x1
    xn_ref[...] = _rms(x1, g_ref[...])


def _outproj(x2, y_lru, y_att, w_out, g):
    t = x2.shape[0]
    row = lambda w: pl.BlockSpec((ROW_TILE, w), lambda i: (i, 0))
    return pl.pallas_call(
        _outproj_kernel,
        out_shape=[jax.ShapeDtypeStruct((t, D_MODEL), F32)] * 2,
        grid=(t // ROW_TILE,),
        in_specs=[row(D_MODEL), row(LRU_WIDTH), row(ATTN_WIDTH), _resident((D_MODEL, D_MODEL)),
                  _resident((1, D_MODEL))],
        out_specs=[row(D_MODEL)] * 2,
        compiler_params=_params("parallel"),
        name="outproj",
    )(x2, y_lru, y_att, w_out.astype(BF16), g.reshape(1, D_MODEL))


def _pair_candidates(s0, i0, s1, i1):
    b_ids = lax.broadcasted_iota(jnp.int32, (SUBLANES,) + s1.shape[1:], 0)
    vals, ids = [], []
    a = 0
    while TOPK // (a + 1) > 1:
        n_b = TOPK // (a + 1)
        n_rows = -(-n_b // SUBLANES) * SUBLANES
        v = s0[a:a + 1] + s1[:n_rows]
        if n_b < n_rows:
            v = jnp.where(b_ids < n_b, v, -jnp.inf)
        vals.append(v)
        ids.append(i0[a:a + 1] * N_KEYS + i1[:n_rows])
        a += 1
    vals.append(s0[a:] + s1[0:1])
    ids.append(i0[a:] * N_KEYS + i1[0:1])
    return jnp.concatenate(vals, axis=0), jnp.concatenate(ids, axis=0)


def _top_rows(s, payload, k):
    n = s.shape[0]
    rows = lax.broadcasted_iota(jnp.int32, s.shape, 0).astype(F32)
    vals, picks = [], []
    for _ in range(k):
        m = jnp.max(s, axis=0, keepdims=True)
        first = jnp.min(jnp.where(s == m, rows, float(n)), axis=0, keepdims=True)
        hit = rows == first
        vals.append(m)
        if payload is None:
            picks.append(first)
        else:
            picks.append(jnp.max(jnp.where(hit, payload, -1.0), axis=0, keepdims=True))
        s = jnp.where(hit, -jnp.inf, s)
    return jnp.concatenate(vals, axis=0), jnp.concatenate(picks, axis=0)


def _peer_topk_kernel(xn_ref, wq_ref, keys_ref, off_ref, g_ref, q_scr, off_scr, g_scr):
    q_t = lax.dot_general(wq_ref[...], xn_ref[...].astype(BF16), (((1,), (1,)), ((), ())),
                          preferred_element_type=F32)
    q_scr[...] = q_t.astype(BF16)

    def head(h, carry):
        tops = []
        for half in range(2):
            hc = h * 2 + half
            q_hc = q_scr[pl.ds(pl.multiple_of(hc * D_HALF, D_HALF), D_HALF), :]
            s = jnp.dot(keys_ref[hc], q_hc, preferred_element_type=F32)
            tops.append(_top_rows(s, None, TOPK))
        (s0, i0), (s1, i1) = tops
        best, idx = _top_rows(*_pair_candidates(s0, i0, s1, i1), TOPK)
        e = jnp.exp(best - best[0:1])
        rows = pl.ds(pl.multiple_of(h * TOPK, TOPK), TOPK)
        off_scr[rows, :] = (idx * WORD_ROWS).astype(jnp.int32)
        g_scr[rows, :] = e / jnp.sum(e, axis=0, keepdims=True)
        return carry

    lax.fori_loop(0, PEER_HEADS, head, 0)
    off_ref[...] = off_scr[...].T
    g_ref[...] = g_scr[...].T


def _peer_topk(xn, w_query, sub_keys):
    t = xn.shape[0]
    nq = PEER_HEADS * D_QUERY
    out_blk = pl.BlockSpec((TOPK_TILE, N_PAIR), lambda i: (i, 0))
    return pl.pallas_call(
        _peer_topk_kernel,
        out_shape=[jax.ShapeDtypeStruct((t, N_PAIR), jnp.int32), jax.ShapeDtypeStruct((t, N_PAIR), F32)],
        grid=(t // TOPK_TILE,),
        in_specs=[pl.BlockSpec((TOPK_TILE, D_MODEL), lambda i: (i, 0)), _resident((nq, D_MODEL)),
                  _resident((2 * PEER_HEADS, N_KEYS, D_HALF))],
        out_specs=[out_blk, out_blk],
        scratch_shapes=[pltpu.VMEM((nq, TOPK_TILE), BF16), pltpu.VMEM((N_PAIR, TOPK_TILE), jnp.int32),
                        pltpu.VMEM((N_PAIR, TOPK_TILE), F32)],
        compiler_params=_params("parallel"),
        name="peer_topk",
    )(xn, w_query.T.astype(BF16), sub_keys.reshape(2 * PEER_HEADS, N_KEYS, D_HALF).astype(BF16))


def _pack_kernel(x_ref, o_ref):
    rows = x_ref.shape[0]
    for j in range(WORD_ROWS):
        lo = x_ref[:, j * LANES:(j + 1) * LANES]
        hi = x_ref[:, (WORD_ROWS + j) * LANES:(WORD_ROWS + j + 1) * LANES]
        o_ref[pl.ds(j, rows, stride=WORD_ROWS), :] = pltpu.pack_elementwise([lo, hi], packed_dtype=BF16)


def _pack_table(tab):
    n = tab.shape[0]
    return pl.pallas_call(
        _pack_kernel,
        out_shape=jax.ShapeDtypeStruct((n * WORD_ROWS, LANES), PACKED),
        grid=(n // PACK_TILE,),
        in_specs=[pl.BlockSpec((PACK_TILE, D_MODEL), lambda i: (i, 0))],
        out_specs=pl.BlockSpec((PACK_TILE * WORD_ROWS, LANES), lambda i: (i, 0)),
        compiler_params=_params("parallel"),
        name="pack_table",
    )(tab)


def _slot_chunk(slot):
    return slot // 2 + WORD_ROWS * (slot % 2)


def _chunk_sum_matrix():
    return (np.arange(D_MODEL)[:, None] // N_CHUNK == np.arange(N_PAIR)[None, :]).astype(np.float32)


def _gathered_tile(off_ref, tab_ref, t, j):
    parts = [tab_ref[pl.ds(pl.multiple_of(off_ref[t, TILE_PAIRS * j + i], WORD_ROWS), WORD_ROWS), :]
             for i in range(TILE_PAIRS)]
    return pltpu.bitcast(jnp.concatenate(parts, axis=0), BF16)


def _select_dot(a, sel):
    out = None
    for _ in range(3):
        piece = a.astype(BF16)
        part = jnp.dot(piece, sel, preferred_element_type=F32)
        out = part if out is None else out + part
        a = a - piece.astype(F32)
    return out


def _swap_tile_rows(tiles):
    tiles = list(tiles)
    rows = lax.broadcasted_iota(jnp.int32, tiles[0].shape, 0)
    d = SUBLANES // 2
    while d:
        low = (rows & d) == 0
        out = list(tiles)
        for i in range(SUBLANES):
            if not i & d:
                a, b = tiles[i], tiles[i + d]
                out[i] = jnp.where(low, a, pltpu.roll(b, d, axis=0))
                out[i + d] = jnp.where(low, pltpu.roll(a, SUBLANES - d, axis=0), b)
        tiles = out
        d //= 2
    return tiles


def _split_bf16(a):
    hi = a.astype(BF16)
    return jnp.concatenate([hi, (a - hi.astype(F32)).astype(BF16)], axis=0)


def _for_each_token(off_now, off_next, off_bufs, sem, compute):
    s = pl.program_id(0)

    def fetch(src, which):
        return pltpu.make_async_copy(src.at[pl.ds(which * PEER_HALF, PEER_HALF)], off_bufs[which], sem.at[which])

    def consume(which):
        for t in range(PEER_HALF):
            compute(which * PEER_HALF + t, off_bufs[which], t)

    not_last = s + 1 < pl.num_programs(0)

    @pl.when(s == 0)
    def _():
        first = fetch(off_now, 0)
        first.start()
        first.wait()

    fetch(off_now, 1).start()
    consume(0)
    fetch(off_now, 1).wait()

    @pl.when(not_last)
    def _():
        fetch(off_next, 0).start()

    consume(1)

    @pl.when(not_last)
    def _():
        fetch(off_next, 0).wait()


def _group_row0():
    return pl.multiple_of((pl.program_id(0) % STEPS_PER_GROUP) * PEER_STEP, PEER_STEP)


def _peer_u_kernel(off_now, off_next, x_ref, tab_ref, sum_ref, gate_ref, w_ref, off_a, off_b, sem, z_scr):
    shape = (2 * N_CHUNK, MXU_ROWS)
    own = (lax.broadcasted_iota(jnp.int32, shape, 0) % N_CHUNK) == _slot_chunk(lax.broadcasted_iota(jnp.int32, shape, 1) % N_CHUNK)
    row0 = _group_row0()

    def compute(tok, off_ref, t):
        lhs = _split_bf16(x_ref[tok])
        zs = []
        for j in range(N_TILES):
            o = lax.dot_general(lhs, _gathered_tile(off_ref, tab_ref, t, j), (((1,), (1,)), ((), ())),
                                preferred_element_type=F32)
            zs.append(jnp.sum(jnp.where(own, o, 0.0), axis=0, keepdims=True))
        z_scr[pl.ds(row0 + tok, 1), :] = jnp.concatenate(zs, axis=1)

    _for_each_token(off_now, off_next, (off_a, off_b), sem, compute)

    @pl.when(pl.program_id(0) % STEPS_PER_GROUP == STEPS_PER_GROUP - 1)
    def _():
        act = _select_dot(z_scr[...], sum_ref[...])
        w_ref[...] = gate_ref[...] * _gelu(act)


def _peer_v_kernel(off_now, off_next, w_ref, x1_ref, tab_ref, spread_ref, y_ref, off_a, off_b, sem, wx_scr):
    shape = (N_CHUNK, D_MODEL)
    own = lax.broadcasted_iota(jnp.int32, shape, 0) == _slot_chunk(lax.broadcasted_iota(jnp.int32, shape, 1) % N_CHUNK)
    row0 = _group_row0()

    @pl.when(pl.program_id(0) % STEPS_PER_GROUP == 0)
    def _():
        wx_scr[...] = _select_dot(w_ref[...], spread_ref[...])

    def compute(tok, off_ref, t):
        lhs = _split_bf16(jnp.where(own, jnp.broadcast_to(wx_scr[pl.ds(row0 + tok, 1), :], shape), 0.0))
        r = None
        for j in range(N_TILES):
            part = jnp.dot(lhs[:, j * MXU_ROWS:(j + 1) * MXU_ROWS], _gathered_tile(off_ref, tab_ref, t, j),
                           preferred_element_type=F32)
            r = part if r is None else r + part
        pending.append(r[:N_CHUNK] + r[N_CHUNK:])
        if len(pending) == SUBLANES:
            by_chunk = _swap_tile_rows(pending)
            pending.clear()
            toks = slice(tok + 1 - SUBLANES, tok + 1)
            for c in range(N_CHUNK):
                cols = slice(c * LANES, (c + 1) * LANES)
                y_ref[toks, cols] = x1_ref[toks, cols] + by_chunk[c]

    pending = []
    _for_each_token(off_now, off_next, (off_a, off_b), sem, compute)


def _peer_specs(n_steps):
    pairs = pl.BlockSpec((PEER_GROUP, N_PAIR), lambda i: (i // STEPS_PER_GROUP, 0))
    rows = pl.BlockSpec((PEER_STEP, N_CHUNK, LANES), lambda i: (i, 0, 0))
    table = _resident((N_EXPERTS * WORD_ROWS, LANES))
    scratch = ([pltpu.SMEM((PEER_HALF, N_PAIR), jnp.int32)] * 2 + [pltpu.SemaphoreType.DMA((2,))]
               + [pltpu.VMEM((PEER_GROUP, D_MODEL), F32)])
    offs = [pl.BlockSpec((PEER_STEP, N_PAIR), lambda i: (i, 0)),
            pl.BlockSpec((PEER_STEP, N_PAIR), lambda i: (jnp.minimum(i + 1, n_steps - 1), 0))]
    return offs, pairs, rows, table, scratch


def _peer_u(off, gate, xn, tab_u):
    t = off.shape[0]
    offs, pairs, rows, table, scratch = _peer_specs(t // PEER_STEP)
    return pl.pallas_call(
        _peer_u_kernel,
        out_shape=jax.ShapeDtypeStruct((t, N_PAIR), F32),
        grid=(t // PEER_STEP,),
        in_specs=offs + [rows, table, _resident((D_MODEL, N_PAIR)), pairs],
        out_specs=pairs,
        scratch_shapes=scratch,
        compiler_params=_params("arbitrary"),
        name="peer_u",
    )(off, off, xn.reshape(t, N_CHUNK, LANES), tab_u, jnp.asarray(_chunk_sum_matrix(), BF16), gate)


def _peer_v(off, w, x1, tab_v):
    t = off.shape[0]
    offs, pairs, rows, table, scratch = _peer_specs(t // PEER_STEP)
    flat = pl.BlockSpec((PEER_STEP, D_MODEL), lambda i: (i, 0))
    return pl.pallas_call(
        _peer_v_kernel,
        out_shape=jax.ShapeDtypeStruct((t, D_MODEL), F32),
        grid=(t // PEER_STEP,),
        in_specs=offs + [pairs, flat, table, _resident((N_PAIR, D_MODEL))],
        out_specs=flat,
        scratch_shapes=scratch,
        compiler_params=_params("arbitrary"),
        name="peer_v",
    )(off, off, w, x1, tab_v, jnp.asarray(_chunk_sum_matrix().T, BF16))


def kernel(x, ln_mix_g, w_in, conv_w, conv_b, w_gate_a, b_gate_a, w_gate_x, b_gate_x, lru_L, q_norm_g,
           k_norm_g, sinks, lru_out_g, attn_out_g, w_out, ln_ffn_g, w_query, sub_keys, expert_u, expert_v,
           rel_bias):
    batch, seq, _ = x.shape
    bias = _band_bias(rel_bias)
    x2 = x.reshape(batch * seq, D_MODEL)
    for l in range(w_in.shape[0]):
        xb, gb, q, k, v = _inproj(x2, ln_mix_g[l], w_in[l])
        y_lru = _rglru(xb, gb, conv_w[l], conv_b[l], w_gate_a[l], b_gate_a[l], w_gate_x[l], b_gate_x[l],
                       lru_L[l], lru_out_g[l], batch, seq)
        y_att = _swa(q, k, v, bias, q_norm_g[l], k_norm_g[l], sinks[l], attn_out_g[l], batch, seq)
        x1, xn = _outproj(x2, y_lru, y_att, w_out[l], ln_ffn_g[l])
        off, gate = _peer_topk(xn, w_query[l], sub_keys[l])
        w = _peer_u(off, gate, xn, _pack_table(expert_u[l]))
        x2 = _peer_v(off, w, x1, _pack_table(expert_v[l]))
    return x2.reshape(batch, seq, D_MODEL)
```

```python
import math

import jax
import jax.numpy as jnp
import numpy as np
from jax import lax
from jax.experimental import pallas as pl
from jax.experimental.pallas import tpu as pltpu

D_MODEL = 1024
LRU_WIDTH = 512
LRU_BLOCKS = 8
LRU_BLOCK = LRU_WIDTH // LRU_BLOCKS
CONV_WIDTH = 4
LRU_C = 8.0
N_HEADS = 8
N_KV_HEADS = 2
GROUP = N_HEADS // N_KV_HEADS
HEAD_DIM = 64
ATTN_WIDTH = N_HEADS * HEAD_DIM
KV_WIDTH = N_KV_HEADS * HEAD_DIM
WINDOW = 128
BLOCK = 128
N_BUCKETS = 32
MAX_DISTANCE = 128
PEER_HEADS = 8
N_KEYS = 128
N_EXPERTS = N_KEYS * N_KEYS
D_QUERY = 256
D_HALF = D_QUERY // 2
TOPK = 16
N_PAIR = PEER_HEADS * TOPK
IN_COLS = 2 * LRU_WIDTH + ATTN_WIDTH + 2 * KV_WIDTH
EPS = 1e-6
NEG_INF = -1e30
SCALE = HEAD_DIM ** -0.5

F32 = jnp.float32
BF16 = jnp.bfloat16
PACKED = jnp.uint32
LANES = 128
SUBLANES = 8
N_CHUNK = D_MODEL // LANES
WORD_ROWS = N_CHUNK // 2
MXU_ROWS = 256
TILE_PAIRS = MXU_ROWS // N_CHUNK
N_TILES = N_PAIR // TILE_PAIRS
VMEM_LIMIT = 52 * 1024 * 1024

ROW_TILE = 1024
PACK_TILE = 2048
LRU_TILE = 512
SWA_BLOCKS = 4
TOPK_TILE = 512
PEER_HALF = 64
PEER_STEP = 2 * PEER_HALF
PEER_GROUP = 256
STEPS_PER_GROUP = PEER_GROUP // PEER_STEP


def _rms(x, g):
    return x * lax.rsqrt(jnp.mean(x * x, axis=-1, keepdims=True) + EPS) * g


def _gelu(x):
    return 0.5 * x * (1.0 + jnp.tanh(math.sqrt(2.0 / math.pi) * (x + 0.044715 * (x * x * x))))


def _params(*sem):
    return pltpu.CompilerParams(dimension_semantics=sem, vmem_limit_bytes=VMEM_LIMIT)


def _resident(shape):
    zeros = (0,) * len(shape)
    return pl.BlockSpec(shape, lambda *_: zeros, pipeline_mode=pl.Buffered(1))


def _inproj_kernel(x_ref, g_ref, w_ref, xb_ref, gb_ref, q_ref, k_ref, v_ref):
    h = _rms(x_ref[...], g_ref[...])
    p = jnp.dot(h.astype(BF16), w_ref[...], preferred_element_type=F32)
    c0, c1, c2, c3 = LRU_WIDTH, 2 * LRU_WIDTH, 2 * LRU_WIDTH + ATTN_WIDTH, IN_COLS - KV_WIDTH
    xb_ref[...] = p[:, :c0]
    gb_ref[...] = p[:, c0:c1]
    q_ref[...] = p[:, c1:c2]
    k_ref[...] = p[:, c2:c3]
    v_ref[...] = p[:, c3:]


def _inproj(x2, g, w_in):
    t = x2.shape[0]
    widths = (LRU_WIDTH, LRU_WIDTH, ATTN_WIDTH, KV_WIDTH, KV_WIDTH)
    row = lambda w: pl.BlockSpec((ROW_TILE, w), lambda i: (i, 0))
    return pl.pallas_call(
        _inproj_kernel,
        out_shape=[jax.ShapeDtypeStruct((t, w), F32) for w in widths],
        grid=(t // ROW_TILE,),
        in_specs=[row(D_MODEL), _resident((1, D_MODEL)), _resident((D_MODEL, IN_COLS))],
        out_specs=[row(w) for w in widths],
        compiler_params=_params("parallel"),
        name="inproj",
    )(x2, g.reshape(1, D_MODEL), w_in.astype(BF16))


def _rglru_kernel(xb_ref, gb_ref, cw_ref, cb_ref, wg_ref, bg_ref, l_ref, og_ref, o_ref,
                  xs_scr, a_scr, b_scr, h_scr):
    tt, c = xb_ref.shape

    @pl.when(pl.program_id(1) == 0)
    def _():
        xs_scr[0:SUBLANES, :] = jnp.zeros((SUBLANES, c), F32)
        h_scr[...] = jnp.zeros_like(h_scr)

    xb = xb_ref[...]
    xs_scr[SUBLANES:SUBLANES + tt, :] = xb
    xc = cb_ref[...] + xb * cw_ref[CONV_WIDTH - 1:CONV_WIDTH, :]
    for back in range(1, CONV_WIDTH):
        tap = CONV_WIDTH - 1 - back
        xc = xc + xs_scr[SUBLANES - back:SUBLANES - back + tt, :] * cw_ref[tap:tap + 1, :]
    xs_scr[0:SUBLANES, :] = xb[tt - SUBLANES:, :]

    gates = jnp.dot(xc.astype(BF16), wg_ref[...], preferred_element_type=F32) + bg_ref[...]
    r = jax.nn.sigmoid(gates[:, :c])
    ig = jax.nn.sigmoid(gates[:, c:])
    lam = l_ref[...]
    softplus_neg = jnp.maximum(-lam, 0.0) + jnp.log1p(jnp.exp(-jnp.abs(lam)))
    log_a = (-LRU_C) * r * softplus_neg
    a = jnp.exp(log_a)
    b = jnp.sqrt(-jnp.tanh(log_a) * (a * a + 1.0)) * (ig * xc)
    a_scr[...] = a
    b_scr[...] = b

    rows = lax.broadcasted_iota(jnp.int32, (SUBLANES, c), 0)

    def tile(i, h):
        off = pl.multiple_of(i * SUBLANES, SUBLANES)
        at = a_scr[pl.ds(off, SUBLANES), :]
        bt = b_scr[pl.ds(off, SUBLANES), :]
        for d in (1, 2, 4):
            keep = rows >= d
            bt = jnp.where(keep, at * pltpu.roll(bt, d, axis=0) + bt, bt)
            at = jnp.where(keep, at * pltpu.roll(at, d, axis=0), at)
        ht = at * h + bt
        b_scr[pl.ds(off, SUBLANES), :] = ht
        return ht[SUBLANES - 1:SUBLANES, :]

    h_scr[...] = lax.fori_loop(0, tt // SUBLANES, tile, h_scr[...])
    y = b_scr[...] * _gelu(gb_ref[...])
    o_ref[...] = _rms(y, og_ref[...]).astype(o_ref.dtype)


def _rglru(xb, gb, conv_w, conv_b, w_gate_a, b_gate_a, w_gate_x, b_gate_x, lru_l, out_g, batch, seq):
    c = LRU_WIDTH
    eye = jnp.eye(LRU_BLOCKS, dtype=F32)
    dense = lambda w: jnp.einsum("nij,nm->nimj", w, eye).reshape(c, c)
    wg = jnp.concatenate([dense(w_gate_a), dense(w_gate_x)], axis=1).astype(BF16)
    bg = jnp.concatenate([b_gate_a.reshape(1, c), b_gate_x.reshape(1, c)], axis=1)
    blk = pl.BlockSpec((None, LRU_TILE, c), lambda b, j: (b, j, 0))
    out = pl.pallas_call(
        _rglru_kernel,
        out_shape=jax.ShapeDtypeStruct((batch, seq, c), BF16),
        grid=(batch, seq // LRU_TILE),
        in_specs=[blk, blk, _resident((CONV_WIDTH, c)), _resident((1, c)), _resident((c, 2 * c)),
                  _resident((1, 2 * c)), _resident((1, c)), _resident((1, c))],
        out_specs=blk,
        scratch_shapes=[pltpu.VMEM((LRU_TILE + SUBLANES, c), F32), pltpu.VMEM((LRU_TILE, c), F32),
                        pltpu.VMEM((LRU_TILE, c), F32), pltpu.VMEM((1, c), F32)],
        compiler_params=_params("parallel", "arbitrary"),
        name="rglru",
    )(xb.reshape(batch, seq, c), gb.reshape(batch, seq, c), conv_w, conv_b.reshape(1, c), wg, bg,
      lru_l.reshape(1, c), out_g.reshape(1, c))
    return out.reshape(batch * seq, c)


def _bias_kernel(rb_ref, onehot_ref, o_ref):
    o_ref[...] = jnp.dot(rb_ref[...], onehot_ref[...], preferred_element_type=F32,
                         precision=lax.Precision.HIGHEST)


def _t5_bucket(rel):
    n = jnp.maximum(rel, 0)
    max_exact = N_BUCKETS // 2
    nf = jnp.maximum(n, 1).astype(F32)
    large = max_exact + jnp.floor(jnp.log(nf / max_exact) / math.log(MAX_DISTANCE / max_exact)
                                  * (N_BUCKETS - max_exact)).astype(jnp.int32)
    large = jnp.minimum(large, N_BUCKETS - 1)
    return jnp.where(n < max_exact, n, large)


def _band_bias(rel_bias):
    i = jnp.arange(BLOCK)[:, None]
    j = jnp.arange(2 * BLOCK)[None, :]
    bucket = _t5_bucket(BLOCK + i - j).reshape(1, -1)
    onehot = (bucket == jnp.arange(N_BUCKETS)[:, None]).astype(F32)
    out = pl.pallas_call(
        _bias_kernel,
        out_shape=jax.ShapeDtypeStruct((N_HEADS, BLOCK * 2 * BLOCK), F32),
        name="band_bias",
    )(rel_bias.astype(F32).T, onehot)
    return out.reshape(N_HEADS, BLOCK, 2 * BLOCK)


def _swa_kernel(sink_ref, q_ref, kc_ref, kp_ref, vc_ref, vp_ref, bias_ref, qg_ref, kg_ref, og_ref, o_ref):
    kk = jnp.concatenate([kp_ref[...], kc_ref[...]], axis=0)
    vv = jnp.concatenate([vp_ref[...], vc_ref[...]], axis=0).astype(BF16)
    k_n = [_rms(kk[:, hk * HEAD_DIM:(hk + 1) * HEAD_DIM], kg_ref[...]).astype(BF16) for hk in range(N_KV_HEADS)]
    shape = (GROUP * BLOCK, 2 * BLOCK)
    qi = lax.broadcasted_iota(jnp.int32, shape, 0) % BLOCK
    kj = lax.broadcasted_iota(jnp.int32, shape, 1)
    chains = [(sub, hk) for sub in range(SWA_BLOCKS) for hk in range(N_KV_HEADS)]
    rows = lambda sub: slice(sub * BLOCK, (sub + 1) * BLOCK)
    keys = lambda sub: slice(sub * BLOCK, (sub + 2) * BLOCK)
    sink, s, m, p, denom, o = {}, {}, {}, {}, {}, {}
    for sub, hk in chains:
        first_key = jnp.where(pl.program_id(1) > 0, 0, BLOCK) if sub == 0 else 0
        valid = (kj > jnp.maximum(qi + (BLOCK - WINDOW), first_key - 1)) & (kj <= qi + BLOCK)
        heads = range(hk * GROUP, (hk + 1) * GROUP)
        q_n = jnp.concatenate([_rms(q_ref[rows(sub), h * HEAD_DIM:(h + 1) * HEAD_DIM], qg_ref[...]).astype(BF16)
                               for h in heads], axis=0)
        sink[sub, hk] = jnp.concatenate([jnp.full((BLOCK, 1), sink_ref[h], F32) for h in heads], axis=0)
        sc = lax.dot_general(q_n, k_n[hk][keys(sub)], (((1,), (1,)), ((), ())), preferred_element_type=F32)
        sc = sc * SCALE + bias_ref[hk * GROUP:(hk + 1) * GROUP].reshape(shape)
        s[sub, hk] = jnp.where(valid, sc, NEG_INF)
    for c in chains:
        m[c] = jnp.maximum(jnp.max(s[c], axis=-1, keepdims=True), sink[c])
    for c in chains:
        p[c] = jnp.exp(s[c] - m[c])
        denom[c] = jnp.sum(p[c], axis=-1, keepdims=True) + jnp.exp(sink[c] - m[c])
    for sub, hk in chains:
        o[sub, hk] = jnp.dot(p[sub, hk].astype(BF16), vv[keys(sub), hk * HEAD_DIM:(hk + 1) * HEAD_DIM],
                             preferred_element_type=F32) / denom[sub, hk]
    for sub in range(SWA_BLOCKS):
        y = jnp.concatenate([o[sub, hk][g * BLOCK:(g + 1) * BLOCK] for hk in range(N_KV_HEADS)
                             for g in range(GROUP)], axis=-1)
        o_ref[rows(sub), :] = _rms(y, og_ref[...]).astype(o_ref.dtype)


def _swa(q, k, v, bias, q_norm_g, k_norm_g, sinks, out_g, batch, seq):
    nb = seq // BLOCK
    cur = lambda w: pl.BlockSpec((None, SWA_BLOCKS * BLOCK, w), lambda b, n: (b, n, 0))
    prev = lambda w: pl.BlockSpec((None, BLOCK, w), lambda b, n: (b, jnp.maximum(SWA_BLOCKS * n - 1, 0), 0))
    k3 = k.reshape(batch, seq, KV_WIDTH)
    v3 = v.reshape(batch, seq, KV_WIDTH)
    out = pl.pallas_call(
        _swa_kernel,
        out_shape=jax.ShapeDtypeStruct((batch, seq, ATTN_WIDTH), BF16),
        grid=(batch, nb // SWA_BLOCKS),
        in_specs=[pl.BlockSpec(memory_space=pltpu.SMEM), cur(ATTN_WIDTH), cur(KV_WIDTH), prev(KV_WIDTH),
                  cur(KV_WIDTH), prev(KV_WIDTH), _resident((N_HEADS, BLOCK, 2 * BLOCK)),
                  _resident((1, HEAD_DIM)), _resident((1, HEAD_DIM)), _resident((1, ATTN_WIDTH))],
        out_specs=cur(ATTN_WIDTH),
        compiler_params=_params("parallel", "parallel"),
        name="swa",
    )(sinks.astype(F32), q.reshape(batch, seq, ATTN_WIDTH), k3, k3, v3, v3, bias,
      q_norm_g.reshape(1, HEAD_DIM), k_norm_g.reshape(1, HEAD_DIM), out_g.reshape(1, ATTN_WIDTH))
    return out.reshape(batch * seq, ATTN_WIDTH)


def _outproj_kernel(x_ref, ml_ref, ma_ref, w_ref, g_ref, x1_ref, xn_ref):
    acc = jnp.dot(ml_ref[...], w_ref[:LRU_WIDTH, :], preferred_element_type=F32)
    acc = acc + jnp.dot(ma_ref[...], w_ref[LRU_WIDTH:, :], preferred_element_type=F32)
    x1 = x_ref[...] + acc
    x1_ref[...] = x1
    xn_ref[...] = _rms(x1, g_ref[...])


def _outproj(x2, y_lru, y_att, w_out, g):
    t = x2.shape[0]
    row = lambda w: pl.BlockSpec((ROW_TILE, w), lambda i: (i, 0))
    return pl.pallas_call(
        _outproj_kernel,
        out_shape=[jax.ShapeDtypeStruct((t, D_MODEL), F32)] * 2,
        grid=(t // ROW_TILE,),
        in_specs=[row(D_MODEL), row(LRU_WIDTH), row(ATTN_WIDTH), _resident((D_MODEL, D_MODEL)),
                  _resident((1, D_MODEL))],
        out_specs=[row(D_MODEL)] * 2,
        compiler_params=_params("parallel"),
        name="outproj",
    )(x2, y_lru, y_att, w_out.astype(BF16), g.reshape(1, D_MODEL))


def _pair_candidates(s0, i0, s1, i1):
    vals, ids = [], []
    for a in range(TOPK):
        n_b = TOPK // (a + 1)
        vals.append(s0[a:a + 1] + s1[:n_b])
        ids.append(i0[a:a + 1] * N_KEYS + i1[:n_b])
    pad = (-sum(v.shape[0] for v in vals)) % SUBLANES
    if pad:
        vals.append(jnp.full((pad,) + s0.shape[1:], -jnp.inf, F32))
        ids.append(jnp.zeros((pad,) + s0.shape[1:], F32))
    return jnp.concatenate(vals, axis=0), jnp.concatenate(ids, axis=0)


def _top_rows(s, payload, k):
    n = s.shape[0]
    rows = lax.broadcasted_iota(jnp.int32, s.shape, 0).astype(F32)
    vals, picks = [], []
    for _ in range(k):
        m = jnp.max(s, axis=0, keepdims=True)
        first = jnp.min(jnp.where(s == m, rows, float(n)), axis=0, keepdims=True)
        hit = rows == first
        vals.append(m)
        if payload is None:
            picks.append(first)
        else:
            picks.append(jnp.max(jnp.where(hit, payload, -1.0), axis=0, keepdims=True))
        s = jnp.where(hit, -jnp.inf, s)
    return jnp.concatenate(vals, axis=0), jnp.concatenate(picks, axis=0)


def _peer_topk_kernel(xn_ref, wq_ref, keys_ref, off_ref, g_ref, q_scr, off_scr, g_scr):
    q_t = lax.dot_general(wq_ref[...], xn_ref[...].astype(BF16), (((1,), (1,)), ((), ())),
                          preferred_element_type=F32)
    q_scr[...] = q_t.astype(BF16)

    def head(h, carry):
        tops = []
        for half in range(2):
            hc = h * 2 + half
            q_hc = q_scr[pl.ds(pl.multiple_of(hc * D_HALF, D_HALF), D_HALF), :]
            s = jnp.dot(keys_ref[hc], q_hc, preferred_element_type=F32)
            tops.append(_top_rows(s, None, TOPK))
        (s0, i0), (s1, i1) = tops
        best, idx = _top_rows(*_pair_candidates(s0, i0, s1, i1), TOPK)
        e = jnp.exp(best - best[0:1])
        rows = pl.ds(pl.multiple_of(h * TOPK, TOPK), TOPK)
        off_scr[rows, :] = (idx * WORD_ROWS).astype(jnp.int32)
        g_scr[rows, :] = e / jnp.sum(e, axis=0, keepdims=True)
        return carry

    lax.fori_loop(0, PEER_HEADS, head, 0)
    off_ref[...] = off_scr[...].T
    g_ref[...] = g_scr[...].T


def _peer_topk(xn, w_query, sub_keys):
    t = xn.shape[0]
    nq = PEER_HEADS * D_QUERY
    out_blk = pl.BlockSpec((TOPK_TILE, N_PAIR), lambda i: (i, 0))
    return pl.pallas_call(
        _peer_topk_kernel,
        out_shape=[jax.ShapeDtypeStruct((t, N_PAIR), jnp.int32), jax.ShapeDtypeStruct((t, N_PAIR), F32)],
        grid=(t // TOPK_TILE,),
        in_specs=[pl.BlockSpec((TOPK_TILE, D_MODEL), lambda i: (i, 0)), _resident((nq, D_MODEL)),
                  _resident((2 * PEER_HEADS, N_KEYS, D_HALF))],
        out_specs=[out_blk, out_blk],
        scratch_shapes=[pltpu.VMEM((nq, TOPK_TILE), BF16), pltpu.VMEM((N_PAIR, TOPK_TILE), jnp.int32),
                        pltpu.VMEM((N_PAIR, TOPK_TILE), F32)],
        compiler_params=_params("parallel"),
        name="peer_topk",
    )(xn, w_query.T.astype(BF16), sub_keys.reshape(2 * PEER_HEADS, N_KEYS, D_HALF).astype(BF16))


def _pack_kernel(x_ref, o_ref):
    rows = x_ref.shape[0]
    for j in range(WORD_ROWS):
        lo = x_ref[:, j * LANES:(j + 1) * LANES]
        hi = x_ref[:, (WORD_ROWS + j) * LANES:(WORD_ROWS + j + 1) * LANES]
        o_ref[pl.ds(j, rows, stride=WORD_ROWS), :] = pltpu.pack_elementwise([lo, hi], packed_dtype=BF16)


def _pack_table(tab):
    n = tab.shape[0]
    return pl.pallas_call(
        _pack_kernel,
        out_shape=jax.ShapeDtypeStruct((n * WORD_ROWS, LANES), PACKED),
        grid=(n // PACK_TILE,),
        in_specs=[pl.BlockSpec((PACK_TILE, D_MODEL), lambda i: (i, 0))],
        out_specs=pl.BlockSpec((PACK_TILE * WORD_ROWS, LANES), lambda i: (i, 0)),
        compiler_params=_params("parallel"),
        name="pack_table",
    )(tab)


def _slot_chunk(slot):
    return slot // 2 + WORD_ROWS * (slot % 2)


def _chunk_sum_matrix():
    return (np.arange(D_MODEL)[:, None] // N_CHUNK == np.arange(N_PAIR)[None, :]).astype(np.float32)


def _gathered_tile(off_ref, tab_ref, t, j):
    parts = [tab_ref[pl.ds(pl.multiple_of(off_ref[t, TILE_PAIRS * j + i], WORD_ROWS), WORD_ROWS), :]
             for i in range(TILE_PAIRS)]
    return pltpu.bitcast(jnp.concatenate(parts, axis=0), BF16)


def _select_dot(a, sel):
    out = None
    for _ in range(3):
        piece = a.astype(BF16)
        part = jnp.dot(piece, sel, preferred_element_type=F32)
        out = part if out is None else out + part
        a = a - piece.astype(F32)
    return out


def _swap_tile_rows(tiles):
    tiles = list(tiles)
    rows = lax.broadcasted_iota(jnp.int32, tiles[0].shape, 0)
    d = SUBLANES // 2
    while d:
        low = (rows & d) == 0
        out = list(tiles)
        for i in range(SUBLANES):
            if not i & d:
                a, b = tiles[i], tiles[i + d]
                out[i] = jnp.where(low, a, pltpu.roll(b, d, axis=0))
                out[i + d] = jnp.where(low, pltpu.roll(a, SUBLANES - d, axis=0), b)
        tiles = out
        d //= 2
    return tiles


def _split_bf16(a):
    hi = a.astype(BF16)
    return jnp.concatenate([hi, (a - hi.astype(F32)).astype(BF16)], axis=0)


def _for_each_token(off_now, off_next, off_bufs, sem, compute):
    s = pl.program_id(0)

    def fetch(src, which):
        return pltpu.make_async_copy(src.at[pl.ds(which * PEER_HALF, PEER_HALF)], off_bufs[which], sem.at[which])

    def consume(which):
        for t in range(PEER_HALF):
            compute(which * PEER_HALF + t, off_bufs[which], t)

    not_last = s + 1 < pl.num_programs(0)

    @pl.when(s == 0)
    def _():
        first = fetch(off_now, 0)
        first.start()
        first.wait()

    fetch(off_now, 1).start()
    consume(0)
    fetch(off_now, 1).wait()

    @pl.when(not_last)
    def _():
        fetch(off_next, 0).start()

    consume(1)

    @pl.when(not_last)
    def _():
        fetch(off_next, 0).wait()


def _group_row0():
    return pl.multiple_of((pl.program_id(0) % STEPS_PER_GROUP) * PEER_STEP, PEER_STEP)


def _peer_u_kernel(off_now, off_next, x_ref, tab_ref, sum_ref, gate_ref, w_ref, off_a, off_b, sem, z_scr):
    shape = (2 * N_CHUNK, MXU_ROWS)
    own = (lax.broadcasted_iota(jnp.int32, shape, 0) % N_CHUNK) == _slot_chunk(lax.broadcasted_iota(jnp.int32, shape, 1) % N_CHUNK)
    row0 = _group_row0()

    def compute(tok, off_ref, t):
        lhs = _split_bf16(x_ref[tok])
        zs = []
        for j in range(N_TILES):
            o = lax.dot_general(lhs, _gathered_tile(off_ref, tab_ref, t, j), (((1,), (1,)), ((), ())),
                                preferred_element_type=F32)
            zs.append(jnp.sum(jnp.where(own, o, 0.0), axis=0, keepdims=True))
        z_scr[pl.ds(row0 + tok, 1), :] = jnp.concatenate(zs, axis=1)

    _for_each_token(off_now, off_next, (off_a, off_b), sem, compute)

    @pl.when(pl.program_id(0) % STEPS_PER_GROUP == STEPS_PER_GROUP - 1)
    def _():
        act = _select_dot(z_scr[...], sum_ref[...])
        w_ref[...] = gate_ref[...] * _gelu(act)


def _peer_v_kernel(off_now, off_next, w_ref, x1_ref, tab_ref, spread_ref, y_ref, off_a, off_b, sem, wx_scr):
    shape = (N_CHUNK, D_MODEL)
    own = lax.broadcasted_iota(jnp.int32, shape, 0) == _slot_chunk(lax.broadcasted_iota(jnp.int32, shape, 1) % N_CHUNK)
    row0 = _group_row0()

    @pl.when(pl.program_id(0) % STEPS_PER_GROUP == 0)
    def _():
        wx_scr[...] = _select_dot(w_ref[...], spread_ref[...])

    def compute(tok, off_ref, t):
        lhs = _split_bf16(jnp.where(own, jnp.broadcast_to(wx_scr[pl.ds(row0 + tok, 1), :], shape), 0.0))
        r = None
        for j in range(N_TILES):
            part = jnp.dot(lhs[:, j * MXU_ROWS:(j + 1) * MXU_ROWS], _gathered_tile(off_ref, tab_ref, t, j),
                           preferred_element_type=F32)
            r = part if r is None else r + part
        pending.append(r[:N_CHUNK] + r[N_CHUNK:])
        if len(pending) == SUBLANES:
            by_chunk = _swap_tile_rows(pending)
            pending.clear()
            toks = slice(tok + 1 - SUBLANES, tok + 1)
            for c in range(N_CHUNK):
                cols = slice(c * LANES, (c + 1) * LANES)
                y_ref[toks, cols] = x1_ref[toks, cols] + by_chunk[c]

    pending = []
    _for_each_token(off_now, off_next, (off_a, off_b), sem, compute)


def _peer_specs(n_steps):
    pairs = pl.BlockSpec((PEER_GROUP, N_PAIR), lambda i: (i // STEPS_PER_GROUP, 0))
    rows = pl.BlockSpec((PEER_STEP, N_CHUNK, LANES), lambda i: (i, 0, 0))
    table = _resident((N_EXPERTS * WORD_ROWS, LANES))
    scratch = ([pltpu.SMEM((PEER_HALF, N_PAIR), jnp.int32)] * 2 + [pltpu.SemaphoreType.DMA((2,))]
               + [pltpu.VMEM((PEER_GROUP, D_MODEL), F32)])
    offs = [pl.BlockSpec((PEER_STEP, N_PAIR), lambda i: (i, 0)),
            pl.BlockSpec((PEER_STEP, N_PAIR), lambda i: (jnp.minimum(i + 1, n_steps - 1), 0))]
    return offs, pairs, rows, table, scratch


def _peer_u(off, gate, xn, tab_u):
    t = off.shape[0]
    offs, pairs, rows, table, scratch = _peer_specs(t // PEER_STEP)
    return pl.pallas_call(
        _peer_u_kernel,
        out_shape=jax.ShapeDtypeStruct((t, N_PAIR), F32),
        grid=(t // PEER_STEP,),
        in_specs=offs + [rows, table, _resident((D_MODEL, N_PAIR)), pairs],
        out_specs=pairs,
        scratch_shapes=scratch,
        compiler_params=_params("arbitrary"),
        name="peer_u",
    )(off, off, xn.reshape(t, N_CHUNK, LANES), tab_u, jnp.asarray(_chunk_sum_matrix(), BF16), gate)


def _peer_v(off, w, x1, tab_v):
    t = off.shape[0]
    offs, pairs, rows, table, scratch = _peer_specs(t // PEER_STEP)
    flat = pl.BlockSpec((PEER_STEP, D_MODEL), lambda i: (i, 0))
    return pl.pallas_call(
        _peer_v_kernel,
        out_shape=jax.ShapeDtypeStruct((t, D_MODEL), F32),
        grid=(t // PEER_STEP,),
        in_specs=offs + [pairs, flat, table, _resident((N_PAIR, D_MODEL))],
        out_specs=flat,
        scratch_shapes=scratch,
        compiler_params=_params("arbitrary"),
        name="peer_v",
    )(off, off, w, x1, tab_v, jnp.asarray(_chunk_sum_matrix().T, BF16))


def kernel(x, ln_mix_g, w_in, conv_w, conv_b, w_gate_a, b_gate_a, w_gate_x, b_gate_x, lru_L, q_norm_g,
           k_norm_g, sinks, lru_out_g, attn_out_g, w_out, ln_ffn_g, w_query, sub_keys, expert_u, expert_v,
           rel_bias):
    batch, seq, _ = x.shape
    bias = _band_bias(rel_bias)
    x2 = x.reshape(batch * seq, D_MODEL)
    for l in range(w_in.shape[0]):
        xb, gb, q, k, v = _inproj(x2, ln_mix_g[l], w_in[l])
        y_lru = _rglru(xb, gb, conv_w[l], conv_b[l], w_gate_a[l], b_gate_a[l], w_gate_x[l], b_gate_x[l],
                       lru_L[l], lru_out_g[l], batch, seq)
        y_att = _swa(q, k, v, bias, q_norm_g[l], k_norm_g[l], sinks[l], attn_out_g[l], batch, seq)
        x1, xn = _outproj(x2, y_lru, y_att, w_out[l], ln_ffn_g[l])
        off, gate = _peer_topk(xn, w_query[l], sub_keys[l])
        w = _peer_u(off, gate, xn, _pack_table(expert_u[l]))
        x2 = _peer_v(off, w, x1, _pack_table(expert_v[l]))
    return x2.reshape(batch, seq, D_MODEL)
```

```python
import math

import jax
import jax.numpy as jnp
import numpy as np
from jax import lax
from jax.experimental import pallas as pl
from jax.experimental.pallas import tpu as pltpu

D_MODEL = 1024
LRU_WIDTH = 512
LRU_BLOCKS = 8
LRU_BLOCK = LRU_WIDTH // LRU_BLOCKS
CONV_WIDTH = 4
LRU_C = 8.0
N_HEADS = 8
N_KV_HEADS = 2
GROUP = N_HEADS // N_KV_HEADS
HEAD_DIM = 64
ATTN_WIDTH = N_HEADS * HEAD_DIM
KV_WIDTH = N_KV_HEADS * HEAD_DIM
WINDOW = 128
BLOCK = 128
N_BUCKETS = 32
MAX_DISTANCE = 128
PEER_HEADS = 8
N_KEYS = 128
N_EXPERTS = N_KEYS * N_KEYS
D_QUERY = 256
D_HALF = D_QUERY // 2
TOPK = 16
N_PAIR = PEER_HEADS * TOPK
IN_COLS = 2 * LRU_WIDTH + ATTN_WIDTH + 2 * KV_WIDTH
EPS = 1e-6
NEG_INF = -1e30
SCALE = HEAD_DIM ** -0.5

F32 = jnp.float32
BF16 = jnp.bfloat16
PACKED = jnp.uint32
LANES = 128
SUBLANES = 8
N_CHUNK = D_MODEL // LANES
WORD_ROWS = N_CHUNK // 2
MXU_ROWS = 256
TILE_PAIRS = MXU_ROWS // N_CHUNK
N_TILES = N_PAIR // TILE_PAIRS
VMEM_LIMIT = 52 * 1024 * 1024

ROW_TILE = 1024
PACK_TILE = 1024
LRU_TILE = 512
SWA_BLOCKS = 4
TOPK_TILE = 512
PEER_HALF = 64
PEER_STEP = 2 * PEER_HALF
PEER_GROUP = 256
STEPS_PER_GROUP = PEER_GROUP // PEER_STEP


def _rms(x, g):
    return x * lax.rsqrt(jnp.mean(x * x, axis=-1, keepdims=True) + EPS) * g


def _gelu(x):
    return 0.5 * x * (1.0 + jnp.tanh(math.sqrt(2.0 / math.pi) * (x + 0.044715 * (x * x * x))))


def _params(*sem):
    return pltpu.CompilerParams(dimension_semantics=sem, vmem_limit_bytes=VMEM_LIMIT)


def _resident(shape):
    zeros = (0,) * len(shape)
    return pl.BlockSpec(shape, lambda *_: zeros, pipeline_mode=pl.Buffered(1))


def _inproj_kernel(x_ref, g_ref, w_ref, xb_ref, gb_ref, q_ref, k_ref, v_ref):
    h = _rms(x_ref[...], g_ref[...])
    p = jnp.dot(h.astype(BF16), w_ref[...], preferred_element_type=F32)
    c0, c1, c2, c3 = LRU_WIDTH, 2 * LRU_WIDTH, 2 * LRU_WIDTH + ATTN_WIDTH, IN_COLS - KV_WIDTH
    xb_ref[...] = p[:, :c0]
    gb_ref[...] = p[:, c0:c1]
    q_ref[...] = p[:, c1:c2]
    k_ref[...] = p[:, c2:c3]
    v_ref[...] = p[:, c3:]


def _inproj(x2, g, w_in):
    t = x2.shape[0]
    widths = (LRU_WIDTH, LRU_WIDTH, ATTN_WIDTH, KV_WIDTH, KV_WIDTH)
    row = lambda w: pl.BlockSpec((ROW_TILE, w), lambda i: (i, 0))
    return pl.pallas_call(
        _inproj_kernel,
        out_shape=[jax.ShapeDtypeStruct((t, w), F32) for w in widths],
        grid=(t // ROW_TILE,),
        in_specs=[row(D_MODEL), _resident((1, D_MODEL)), _resident((D_MODEL, IN_COLS))],
        out_specs=[row(w) for w in widths],
        compiler_params=_params("parallel"),
        name="inproj",
    )(x2, g.reshape(1, D_MODEL), w_in.astype(BF16))


def _rglru_kernel(xb_ref, gb_ref, cw_ref, cb_ref, wg_ref, bg_ref, l_ref, og_ref, o_ref,
                  xs_scr, a_scr, b_scr, h_scr):
    tt, c = xb_ref.shape

    @pl.when(pl.program_id(1) == 0)
    def _():
        xs_scr[0:SUBLANES, :] = jnp.zeros((SUBLANES, c), F32)
        h_scr[...] = jnp.zeros_like(h_scr)

    xb = xb_ref[...]
    xs_scr[SUBLANES:SUBLANES + tt, :] = xb
    xc = cb_ref[...] + xb * cw_ref[CONV_WIDTH - 1:CONV_WIDTH, :]
    for back in range(1, CONV_WIDTH):
        tap = CONV_WIDTH - 1 - back
        xc = xc + xs_scr[SUBLANES - back:SUBLANES - back + tt, :] * cw_ref[tap:tap + 1, :]
    xs_scr[0:SUBLANES, :] = xb[tt - SUBLANES:, :]

    gates = jnp.dot(xc.astype(BF16), wg_ref[...], preferred_element_type=F32) + bg_ref[...]
    r = jax.nn.sigmoid(gates[:, :c])
    ig = jax.nn.sigmoid(gates[:, c:])
    lam = l_ref[...]
    softplus_neg = jnp.maximum(-lam, 0.0) + jnp.log1p(jnp.exp(-jnp.abs(lam)))
    log_a = (-LRU_C) * r * softplus_neg
    a = jnp.exp(log_a)
    b = jnp.sqrt(-jnp.tanh(log_a) * (a * a + 1.0)) * (ig * xc)
    a_scr[...] = a
    b_scr[...] = b

    rows = lax.broadcasted_iota(jnp.int32, (SUBLANES, c), 0)

    def tile(i, h):
        off = pl.multiple_of(i * SUBLANES, SUBLANES)
        at = a_scr[pl.ds(off, SUBLANES), :]
        bt = b_scr[pl.ds(off, SUBLANES), :]
        for d in (1, 2, 4):
            keep = rows >= d
            bt = jnp.where(keep, at * pltpu.roll(bt, d, axis=0) + bt, bt)
            at = jnp.where(keep, at * pltpu.roll(at, d, axis=0), at)
        ht = at * h + bt
        b_scr[pl.ds(off, SUBLANES), :] = ht
        return ht[SUBLANES - 1:SUBLANES, :]

    h_scr[...] = lax.fori_loop(0, tt // SUBLANES, tile, h_scr[...])
    y = b_scr[...] * _gelu(gb_ref[...])
    o_ref[...] = _rms(y, og_ref[...]).astype(o_ref.dtype)


def _rglru(xb, gb, conv_w, conv_b, w_gate_a, b_gate_a, w_gate_x, b_gate_x, lru_l, out_g, batch, seq):
    c = LRU_WIDTH
    eye = jnp.eye(LRU_BLOCKS, dtype=F32)
    dense = lambda w: jnp.einsum("nij,nm->nimj", w, eye).reshape(c, c)
    wg = jnp.concatenate([dense(w_gate_a), dense(w_gate_x)], axis=1).astype(BF16)
    bg = jnp.concatenate([b_gate_a.reshape(1, c), b_gate_x.reshape(1, c)], axis=1)
    blk = pl.BlockSpec((None, LRU_TILE, c), lambda b, j: (b, j, 0))
    out = pl.pallas_call(
        _rglru_kernel,
        out_shape=jax.ShapeDtypeStruct((batch, seq, c), BF16),
        grid=(batch, seq // LRU_TILE),
        in_specs=[blk, blk, _resident((CONV_WIDTH, c)), _resident((1, c)), _resident((c, 2 * c)),
                  _resident((1, 2 * c)), _resident((1, c)), _resident((1, c))],
        out_specs=blk,
        scratch_shapes=[pltpu.VMEM((LRU_TILE + SUBLANES, c), F32), pltpu.VMEM((LRU_TILE, c), F32),
                        pltpu.VMEM((LRU_TILE, c), F32), pltpu.VMEM((1, c), F32)],
        compiler_params=_params("parallel", "arbitrary"),
        name="rglru",
    )(xb.reshape(batch, seq, c), gb.reshape(batch, seq, c), conv_w, conv_b.reshape(1, c), wg, bg,
      lru_l.reshape(1, c), out_g.reshape(1, c))
    return out.reshape(batch * seq, c)


def _bias_kernel(rb_ref, onehot_ref, o_ref):
    o_ref[...] = jnp.dot(rb_ref[...], onehot_ref[...], preferred_element_type=F32,
                         precision=lax.Precision.HIGHEST)


def _t5_bucket(rel):
    n = jnp.maximum(rel, 0)
    max_exact = N_BUCKETS // 2
    nf = jnp.maximum(n, 1).astype(F32)
    large = max_exact + jnp.floor(jnp.log(nf / max_exact) / math.log(MAX_DISTANCE / max_exact)
                                  * (N_BUCKETS - max_exact)).astype(jnp.int32)
    large = jnp.minimum(large, N_BUCKETS - 1)
    return jnp.where(n < max_exact, n, large)


def _band_bias(rel_bias):
    i = jnp.arange(BLOCK)[:, None]
    j = jnp.arange(2 * BLOCK)[None, :]
    bucket = _t5_bucket(BLOCK + i - j).reshape(1, -1)
    onehot = (bucket == jnp.arange(N_BUCKETS)[:, None]).astype(F32)
    out = pl.pallas_call(
        _bias_kernel,
        out_shape=jax.ShapeDtypeStruct((N_HEADS, BLOCK * 2 * BLOCK), F32),
        name="band_bias",
    )(rel_bias.astype(F32).T, onehot)
    return out.reshape(N_HEADS, BLOCK, 2 * BLOCK)


def _swa_kernel(sink_ref, q_ref, kc_ref, kp_ref, vc_ref, vp_ref, bias_ref, qg_ref, kg_ref, og_ref, o_ref):
    kk = jnp.concatenate([kp_ref[...], kc_ref[...]], axis=0)
    vv = jnp.concatenate([vp_ref[...], vc_ref[...]], axis=0).astype(BF16)
    k_n = [_rms(kk[:, hk * HEAD_DIM:(hk + 1) * HEAD_DIM], kg_ref[...]).astype(BF16) for hk in range(N_KV_HEADS)]
    shape = (GROUP * BLOCK, 2 * BLOCK)
    qi = lax.broadcasted_iota(jnp.int32, shape, 0) % BLOCK
    kj = lax.broadcasted_iota(jnp.int32, shape, 1)
    chains = [(sub, hk) for sub in range(SWA_BLOCKS) for hk in range(N_KV_HEADS)]
    rows = lambda sub: slice(sub * BLOCK, (sub + 1) * BLOCK)
    keys = lambda sub: slice(sub * BLOCK, (sub + 2) * BLOCK)
    sink, s, m, p, denom, o = {}, {}, {}, {}, {}, {}
    for sub, hk in chains:
        first_key = jnp.where(pl.program_id(1) > 0, 0, BLOCK) if sub == 0 else 0
        valid = (kj > jnp.maximum(qi + (BLOCK - WINDOW), first_key - 1)) & (kj <= qi + BLOCK)
        heads = range(hk * GROUP, (hk + 1) * GROUP)
        q_n = jnp.concatenate([_rms(q_ref[rows(sub), h * HEAD_DIM:(h + 1) * HEAD_DIM], qg_ref[...]).astype(BF16)
                               for h in heads], axis=0)
        sink[sub, hk] = jnp.concatenate([jnp.full((BLOCK, 1), sink_ref[h], F32) for h in heads], axis=0)
        sc = lax.dot_general(q_n, k_n[hk][keys(sub)], (((1,), (1,)), ((), ())), preferred_element_type=F32)
        sc = sc * SCALE + bias_ref[hk * GROUP:(hk + 1) * GROUP].reshape(shape)
        s[sub, hk] = jnp.where(valid, sc, NEG_INF)
    for c in chains:
        m[c] = jnp.maximum(jnp.max(s[c], axis=-1, keepdims=True), sink[c])
    for c in chains:
        p[c] = jnp.exp(s[c] - m[c])
        denom[c] = jnp.sum(p[c], axis=-1, keepdims=True) + jnp.exp(sink[c] - m[c])
    for sub, hk in chains:
        o[sub, hk] = jnp.dot(p[sub, hk].astype(BF16), vv[keys(sub), hk * HEAD_DIM:(hk + 1) * HEAD_DIM],
                             preferred_element_type=F32) / denom[sub, hk]
    for sub in range(SWA_BLOCKS):
        y = jnp.concatenate([o[sub, hk][g * BLOCK:(g + 1) * BLOCK] for hk in range(N_KV_HEADS)
                             for g in range(GROUP)], axis=-1)
        o_ref[rows(sub), :] = _rms(y, og_ref[...]).astype(o_ref.dtype)


def _swa(q, k, v, bias, q_norm_g, k_norm_g, sinks, out_g, batch, seq):
    nb = seq // BLOCK
    cur = lambda w: pl.BlockSpec((None, SWA_BLOCKS * BLOCK, w), lambda b, n: (b, n, 0))
    prev = lambda w: pl.BlockSpec((None, BLOCK, w), lambda b, n: (b, jnp.maximum(SWA_BLOCKS * n - 1, 0), 0))
    k3 = k.reshape(batch, seq, KV_WIDTH)
    v3 = v.reshape(batch, seq, KV_WIDTH)
    out = pl.pallas_call(
        _swa_kernel,
        out_shape=jax.ShapeDtypeStruct((batch, seq, ATTN_WIDTH), BF16),
        grid=(batch, nb // SWA_BLOCKS),
        in_specs=[pl.BlockSpec(memory_space=pltpu.SMEM), cur(ATTN_WIDTH), cur(KV_WIDTH), prev(KV_WIDTH),
                  cur(KV_WIDTH), prev(KV_WIDTH), _resident((N_HEADS, BLOCK, 2 * BLOCK)),
                  _resident((1, HEAD_DIM)), _resident((1, HEAD_DIM)), _resident((1, ATTN_WIDTH))],
        out_specs=cur(ATTN_WIDTH),
        compiler_params=_params("parallel", "parallel"),
        name="swa",
    )(sinks.astype(F32), q.reshape(batch, seq, ATTN_WIDTH), k3, k3, v3, v3, bias,
      q_norm_g.reshape(1, HEAD_DIM), k_norm_g.reshape(1, HEAD_DIM), out_g.reshape(1, ATTN_WIDTH))
    return out.reshape(batch * seq, ATTN_WIDTH)


def _outproj_kernel(x_ref, ml_ref, ma_ref, w_ref, g_ref, x1_ref, xn_ref):
    acc = jnp.dot(ml_ref[...], w_ref[:LRU_WIDTH, :], preferred_element_type=F32)
    acc = acc + jnp.dot(ma_ref[...], w_ref[LRU_WIDTH:, :], preferred_element_type=F32)
    x1 = x_ref[...] + acc
    x1_ref[...] = x1
    xn_ref[...] = _rms(x1, g_ref[...])


def _outproj(x2, y_lru, y_att, w_out, g):
    t = x2.shape[0]
    row = lambda w: pl.BlockSpec((ROW_TILE, w), lambda i: (i, 0))
    return pl.pallas_call(
        _outproj_kernel,
        out_shape=[jax.ShapeDtypeStruct((t, D_MODEL), F32)] * 2,
        grid=(t // ROW_TILE,),
        in_specs=[row(D_MODEL), row(LRU_WIDTH), row(ATTN_WIDTH), _resident((D_MODEL, D_MODEL)),
                  _resident((1, D_MODEL))],
        out_specs=[row(D_MODEL)] * 2,
        compiler_params=_params("parallel"),
        name="outproj",
    )(x2, y_lru, y_att, w_out.astype(BF16), g.reshape(1, D_MODEL))


def _pair_candidates(s0, i0, s1, i1):
    vals, ids = [], []
    for a in range(TOPK):
        n_b = TOPK // (a + 1)
        vals.append(s0[a:a + 1] + s1[:n_b])
        ids.append(i0[a:a + 1] * N_KEYS + i1[:n_b])
    pad = (-sum(v.shape[0] for v in vals)) % SUBLANES
    if pad:
        vals.append(jnp.full((pad,) + s0.shape[1:], -jnp.inf, F32))
        ids.append(jnp.zeros((pad,) + s0.shape[1:], F32))
    return jnp.concatenate(vals, axis=0), jnp.concatenate(ids, axis=0)


def _top_rows(s, payload, k):
    n = s.shape[0]
    rows = lax.broadcasted_iota(jnp.int32, s.shape, 0).astype(F32)
    vals, picks = [], []
    for _ in range(k):
        m = jnp.max(s, axis=0, keepdims=True)
        first = jnp.min(jnp.where(s == m, rows, float(n)), axis=0, keepdims=True)
        hit = rows == first
        vals.append(m)
        if payload is None:
            picks.append(first)
        else:
            picks.append(jnp.max(jnp.where(hit, payload, -1.0), axis=0, keepdims=True))
        s = jnp.where(hit, -jnp.inf, s)
    return jnp.concatenate(vals, axis=0), jnp.concatenate(picks, axis=0)


def _peer_topk_kernel(xn_ref, wq_ref, keys_ref, off_ref, g_ref, q_scr, off_scr, g_scr):
    q_t = lax.dot_general(wq_ref[...], xn_ref[...].astype(BF16), (((1,), (1,)), ((), ())),
                          preferred_element_type=F32)
    q_scr[...] = q_t.astype(BF16)

    def head(h, carry):
        tops = []
        for half in range(2):
            hc = h * 2 + half
            q_hc = q_scr[pl.ds(pl.multiple_of(hc * D_HALF, D_HALF), D_HALF), :]
            s = jnp.dot(keys_ref[hc], q_hc, preferred_element_type=F32)
            tops.append(_top_rows(s, None, TOPK))
        (s0, i0), (s1, i1) = tops
        best, idx = _top_rows(*_pair_candidates(s0, i0, s1, i1), TOPK)
        e = jnp.exp(best - best[0:1])
        rows = pl.ds(pl.multiple_of(h * TOPK, TOPK), TOPK)
        off_scr[rows, :] = (idx * WORD_ROWS).astype(jnp.int32)
        g_scr[rows, :] = e / jnp.sum(e, axis=0, keepdims=True)
        return carry

    lax.fori_loop(0, PEER_HEADS, head, 0)
    off_ref[...] = off_scr[...].T
    g_ref[...] = g_scr[...].T


def _peer_topk(xn, w_query, sub_keys):
    t = xn.shape[0]
    nq = PEER_HEADS * D_QUERY
    out_blk = pl.BlockSpec((TOPK_TILE, N_PAIR), lambda i: (i, 0))
    return pl.pallas_call(
        _peer_topk_kernel,
        out_shape=[jax.ShapeDtypeStruct((t, N_PAIR), jnp.int32), jax.ShapeDtypeStruct((t, N_PAIR), F32)],
        grid=(t // TOPK_TILE,),
        in_specs=[pl.BlockSpec((TOPK_TILE, D_MODEL), lambda i: (i, 0)), _resident((nq, D_MODEL)),
                  _resident((2 * PEER_HEADS, N_KEYS, D_HALF))],
        out_specs=[out_blk, out_blk],
        scratch_shapes=[pltpu.VMEM((nq, TOPK_TILE), BF16), pltpu.VMEM((N_PAIR, TOPK_TILE), jnp.int32),
                        pltpu.VMEM((N_PAIR, TOPK_TILE), F32)],
        compiler_params=_params("parallel"),
        name="peer_topk",
    )(xn, w_query.T.astype(BF16), sub_keys.reshape(2 * PEER_HEADS, N_KEYS, D_HALF).astype(BF16))


def _pack_kernel(*refs):
    n_tables = len(refs) // 2
    for x_ref, o_ref in zip(refs[:n_tables], refs[n_tables:]):
        rows = x_ref.shape[0]
        for j in range(WORD_ROWS):
            lo = x_ref[:, j * LANES:(j + 1) * LANES]
            hi = x_ref[:, (WORD_ROWS + j) * LANES:(WORD_ROWS + j + 1) * LANES]
            o_ref[pl.ds(j, rows, stride=WORD_ROWS), :] = pltpu.pack_elementwise([lo, hi], packed_dtype=BF16)


def _pack_tables(*tabs):
    n = tabs[0].shape[0]
    return pl.pallas_call(
        _pack_kernel,
        out_shape=[jax.ShapeDtypeStruct((n * WORD_ROWS, LANES), PACKED)] * len(tabs),
        grid=(n // PACK_TILE,),
        in_specs=[pl.BlockSpec((PACK_TILE, D_MODEL), lambda i: (i, 0))] * len(tabs),
        out_specs=[pl.BlockSpec((PACK_TILE * WORD_ROWS, LANES), lambda i: (i, 0))] * len(tabs),
        compiler_params=_params("parallel"),
        name="pack_tables",
    )(*tabs)


def _slot_chunk(slot):
    return slot // 2 + WORD_ROWS * (slot % 2)


def _chunk_sum_matrix():
    return (np.arange(D_MODEL)[:, None] // N_CHUNK == np.arange(N_PAIR)[None, :]).astype(np.float32)


def _gathered_tile(off_ref, tab_ref, t, j):
    parts = [tab_ref[pl.ds(pl.multiple_of(off_ref[t, TILE_PAIRS * j + i], WORD_ROWS), WORD_ROWS), :]
             for i in range(TILE_PAIRS)]
    return pltpu.bitcast(jnp.concatenate(parts, axis=0), BF16)


def _select_dot(a, sel):
    out = None
    for _ in range(3):
        piece = a.astype(BF16)
        part = jnp.dot(piece, sel, preferred_element_type=F32)
        out = part if out is None else out + part
        a = a - piece.astype(F32)
    return out


def _swap_tile_rows(tiles):
    tiles = list(tiles)
    rows = lax.broadcasted_iota(jnp.int32, tiles[0].shape, 0)
    d = SUBLANES // 2
    while d:
        low = (rows & d) == 0
        out = list(tiles)
        for i in range(SUBLANES):
            if not i & d:
                a, b = tiles[i], tiles[i + d]
                out[i] = jnp.where(low, a, pltpu.roll(b, d, axis=0))
                out[i + d] = jnp.where(low, pltpu.roll(a, SUBLANES - d, axis=0), b)
        tiles = out
        d //= 2
    return tiles


def _split_bf16(a):
    hi = a.astype(BF16)
    return jnp.concatenate([hi, (a - hi.astype(F32)).astype(BF16)], axis=0)


def _for_each_token(off_now, off_next, off_bufs, sem, compute):
    s = pl.program_id(0)

    def fetch(src, which):
        return pltpu.make_async_copy(src.at[pl.ds(which * PEER_HALF, PEER_HALF)], off_bufs[which], sem.at[which])

    def consume(which):
        for t in range(PEER_HALF):
            compute(which * PEER_HALF + t, off_bufs[which], t)

    not_last = s + 1 < pl.num_programs(0)

    @pl.when(s == 0)
    def _():
        first = fetch(off_now, 0)
        first.start()
        first.wait()

    fetch(off_now, 1).start()
    consume(0)
    fetch(off_now, 1).wait()

    @pl.when(not_last)
    def _():
        fetch(off_next, 0).start()

    consume(1)

    @pl.when(not_last)
    def _():
        fetch(off_next, 0).wait()


def _group_row0():
    return pl.multiple_of((pl.program_id(0) % STEPS_PER_GROUP) * PEER_STEP, PEER_STEP)


def _peer_u_kernel(off_now, off_next, x_ref, tab_ref, sum_ref, gate_ref, w_ref, off_a, off_b, sem, z_scr):
    shape = (2 * N_CHUNK, MXU_ROWS)
    own = (lax.broadcasted_iota(jnp.int32, shape, 0) % N_CHUNK) == _slot_chunk(lax.broadcasted_iota(jnp.int32, shape, 1) % N_CHUNK)
    row0 = _group_row0()

    def compute(tok, off_ref, t):
        lhs = _split_bf16(x_ref[tok])
        zs = []
        for j in range(N_TILES):
            o = lax.dot_general(lhs, _gathered_tile(off_ref, tab_ref, t, j), (((1,), (1,)), ((), ())),
                                preferred_element_type=F32)
            zs.append(jnp.sum(jnp.where(own, o, 0.0), axis=0, keepdims=True))
        z_scr[pl.ds(row0 + tok, 1), :] = jnp.concatenate(zs, axis=1)

    _for_each_token(off_now, off_next, (off_a, off_b), sem, compute)

    @pl.when(pl.program_id(0) % STEPS_PER_GROUP == STEPS_PER_GROUP - 1)
    def _():
        act = _select_dot(z_scr[...], sum_ref[...])
        w_ref[...] = gate_ref[...] * _gelu(act)


def _peer_v_kernel(off_now, off_next, w_ref, x1_ref, tab_ref, spread_ref, y_ref, off_a, off_b, sem, wx_scr):
    shape = (N_CHUNK, D_MODEL)
    own = lax.broadcasted_iota(jnp.int32, shape, 0) == _slot_chunk(lax.broadcasted_iota(jnp.int32, shape, 1) % N_CHUNK)
    row0 = _group_row0()

    @pl.when(pl.program_id(0) % STEPS_PER_GROUP == 0)
    def _():
        wx_scr[...] = _select_dot(w_ref[...], spread_ref[...])

    def compute(tok, off_ref, t):
        lhs = _split_bf16(jnp.where(own, jnp.broadcast_to(wx_scr[pl.ds(row0 + tok, 1), :], shape), 0.0))
        r = None
        for j in range(N_TILES):
            part = jnp.dot(lhs[:, j * MXU_ROWS:(j + 1) * MXU_ROWS], _gathered_tile(off_ref, tab_ref, t, j),
                           preferred_element_type=F32)
            r = part if r is None else r + part
        pending.append(r[:N_CHUNK] + r[N_CHUNK:])
        if len(pending) == SUBLANES:
            by_chunk = _swap_tile_rows(pending)
            pending.clear()
            toks = slice(tok + 1 - SUBLANES, tok + 1)
            for c in range(N_CHUNK):
                cols = slice(c * LANES, (c + 1) * LANES)
                y_ref[toks, cols] = x1_ref[toks, cols] + by_chunk[c]

    pending = []
    _for_each_token(off_now, off_next, (off_a, off_b), sem, compute)


def _peer_specs(n_steps):
    pairs = pl.BlockSpec((PEER_GROUP, N_PAIR), lambda i: (i // STEPS_PER_GROUP, 0))
    rows = pl.BlockSpec((PEER_STEP, N_CHUNK, LANES), lambda i: (i, 0, 0))
    table = _resident((N_EXPERTS * WORD_ROWS, LANES))
    scratch = ([pltpu.SMEM((PEER_HALF, N_PAIR), jnp.int32)] * 2 + [pltpu.SemaphoreType.DMA((2,))]
               + [pltpu.VMEM((PEER_GROUP, D_MODEL), F32)])
    offs = [pl.BlockSpec((PEER_STEP, N_PAIR), lambda i: (i, 0)),
            pl.BlockSpec((PEER_STEP, N_PAIR), lambda i: (jnp.minimum(i + 1, n_steps - 1), 0))]
    return offs, pairs, rows, table, scratch


def _peer_u(off, gate, xn, tab_u):
    t = off.shape[0]
    offs, pairs, rows, table, scratch = _peer_specs(t // PEER_STEP)
    return pl.pallas_call(
        _peer_u_kernel,
        out_shape=jax.ShapeDtypeStruct((t, N_PAIR), F32),
        grid=(t // PEER_STEP,),
        in_specs=offs + [rows, table, _resident((D_MODEL, N_PAIR)), pairs],
        out_specs=pairs,
        scratch_shapes=scratch,
        compiler_params=_params("arbitrary"),
        name="peer_u",
    )(off, off, xn.reshape(t, N_CHUNK, LANES), tab_u, jnp.asarray(_chunk_sum_matrix(), BF16), gate)


def _peer_v(off, w, x1, tab_v):
    t = off.shape[0]
    offs, pairs, rows, table, scratch = _peer_specs(t // PEER_STEP)
    flat = pl.BlockSpec((PEER_STEP, D_MODEL), lambda i: (i, 0))
    return pl.pallas_call(
        _peer_v_kernel,
        out_shape=jax.ShapeDtypeStruct((t, D_MODEL), F32),
        grid=(t // PEER_STEP,),
        in_specs=offs + [pairs, flat, table, _resident((N_PAIR, D_MODEL))],
        out_specs=flat,
        scratch_shapes=scratch,
        compiler_params=_params("arbitrary"),
        name="peer_v",
    )(off, off, w, x1, tab_v, jnp.asarray(_chunk_sum_matrix().T, BF16))


def kernel(x, ln_mix_g, w_in, conv_w, conv_b, w_gate_a, b_gate_a, w_gate_x, b_gate_x, lru_L, q_norm_g,
           k_norm_g, sinks, lru_out_g, attn_out_g, w_out, ln_ffn_g, w_query, sub_keys, expert_u, expert_v,
           rel_bias):
    batch, seq, _ = x.shape
    bias = _band_bias(rel_bias)
    x2 = x.reshape(batch * seq, D_MODEL)
    for l in range(w_in.shape[0]):
        xb, gb, q, k, v = _inproj(x2, ln_mix_g[l], w_in[l])
        y_lru = _rglru(xb, gb, conv_w[l], conv_b[l], w_gate_a[l], b_gate_a[l], w_gate_x[l], b_gate_x[l],
                       lru_L[l], lru_out_g[l], batch, seq)
        y_att = _swa(q, k, v, bias, q_norm_g[l], k_norm_g[l], sinks[l], attn_out_g[l], batch, seq)
        x1, xn = _outproj(x2, y_lru, y_att, w_out[l], ln_ffn_g[l])
        off, gate = _peer_topk(xn, w_query[l], sub_keys[l])
        tab_u, tab_v = _pack_tables(expert_u[l], expert_v[l])
        w = _peer_u(off, gate, xn, tab_u)
        x2 = _peer_v(off, w, x1, tab_v)
    return x2.reshape(batch, seq, D_MODEL)
```

```python
import math

import jax
import jax.numpy as jnp
import numpy as np
from jax import lax
from jax.experimental import pallas as pl
from jax.experimental.pallas import tpu as pltpu

D_MODEL = 1024
LRU_WIDTH = 512
LRU_BLOCKS = 8
LRU_BLOCK = LRU_WIDTH // LRU_BLOCKS
CONV_WIDTH = 4
LRU_C = 8.0
N_HEADS = 8
N_KV_HEADS = 2
GROUP = N_HEADS // N_KV_HEADS
HEAD_DIM = 64
ATTN_WIDTH = N_HEADS * HEAD_DIM
KV_WIDTH = N_KV_HEADS * HEAD_DIM
WINDOW = 128
BLOCK = 128
N_BUCKETS = 32
MAX_DISTANCE = 128
PEER_HEADS = 8
N_KEYS = 128
N_EXPERTS = N_KEYS * N_KEYS
D_QUERY = 256
D_HALF = D_QUERY // 2
TOPK = 16
N_PAIR = PEER_HEADS * TOPK
IN_COLS = 2 * LRU_WIDTH + ATTN_WIDTH + 2 * KV_WIDTH
EPS = 1e-6
NEG_INF = -1e30
SCALE = HEAD_DIM ** -0.5

F32 = jnp.float32
BF16 = jnp.bfloat16
PACKED = jnp.uint32
LANES = 128
SUBLANES = 8
N_CHUNK = D_MODEL // LANES
WORD_ROWS = N_CHUNK // 2
MXU_ROWS = 256
TILE_PAIRS = MXU_ROWS // N_CHUNK
N_TILES = N_PAIR // TILE_PAIRS
VMEM_LIMIT = 52 * 1024 * 1024

ROW_TILE = 1024
PACK_TILE = 1024
LRU_TILE = 512
SWA_BLOCKS = 4
TOPK_TILE = 1024
PEER_HALF = 64
PEER_STEP = 2 * PEER_HALF
PEER_GROUP = 256
STEPS_PER_GROUP = PEER_GROUP // PEER_STEP


def _rms(x, g):
    return x * lax.rsqrt(jnp.mean(x * x, axis=-1, keepdims=True) + EPS) * g


def _gelu(x):
    return 0.5 * x * (1.0 + jnp.tanh(math.sqrt(2.0 / math.pi) * (x + 0.044715 * (x * x * x))))


def _params(*sem):
    return pltpu.CompilerParams(dimension_semantics=sem, vmem_limit_bytes=VMEM_LIMIT)


def _resident(shape):
    zeros = (0,) * len(shape)
    return pl.BlockSpec(shape, lambda *_: zeros, pipeline_mode=pl.Buffered(1))


def _inproj_kernel(x_ref, g_ref, w_ref, xb_ref, gb_ref, q_ref, k_ref, v_ref):
    h = _rms(x_ref[...], g_ref[...])
    p = jnp.dot(h.astype(BF16), w_ref[...], preferred_element_type=F32)
    c0, c1, c2, c3 = LRU_WIDTH, 2 * LRU_WIDTH, 2 * LRU_WIDTH + ATTN_WIDTH, IN_COLS - KV_WIDTH
    xb_ref[...] = p[:, :c0]
    gb_ref[...] = p[:, c0:c1]
    q_ref[...] = p[:, c1:c2]
    k_ref[...] = p[:, c2:c3]
    v_ref[...] = p[:, c3:]


def _inproj(x2, g, w_in):
    t = x2.shape[0]
    widths = (LRU_WIDTH, LRU_WIDTH, ATTN_WIDTH, KV_WIDTH, KV_WIDTH)
    row = lambda w: pl.BlockSpec((ROW_TILE, w), lambda i: (i, 0))
    return pl.pallas_call(
        _inproj_kernel,
        out_shape=[jax.ShapeDtypeStruct((t, w), F32) for w in widths],
        grid=(t // ROW_TILE,),
        in_specs=[row(D_MODEL), _resident((1, D_MODEL)), _resident((D_MODEL, IN_COLS))],
        out_specs=[row(w) for w in widths],
        compiler_params=_params("parallel"),
        name="inproj",
    )(x2, g.reshape(1, D_MODEL), w_in.astype(BF16))


def _rglru_kernel(xb_ref, gb_ref, cw_ref, cb_ref, wg_ref, bg_ref, l_ref, og_ref, o_ref,
                  xs_scr, a_scr, b_scr, h_scr):
    tt, c = xb_ref.shape

    @pl.when(pl.program_id(1) == 0)
    def _():
        xs_scr[0:SUBLANES, :] = jnp.zeros((SUBLANES, c), F32)
        h_scr[...] = jnp.zeros_like(h_scr)

    xb = xb_ref[...]
    xs_scr[SUBLANES:SUBLANES + tt, :] = xb
    xc = cb_ref[...] + xb * cw_ref[CONV_WIDTH - 1:CONV_WIDTH, :]
    for back in range(1, CONV_WIDTH):
        tap = CONV_WIDTH - 1 - back
        xc = xc + xs_scr[SUBLANES - back:SUBLANES - back + tt, :] * cw_ref[tap:tap + 1, :]
    xs_scr[0:SUBLANES, :] = xb[tt - SUBLANES:, :]

    gates = jnp.dot(xc.astype(BF16), wg_ref[...], preferred_element_type=F32) + bg_ref[...]
    r = jax.nn.sigmoid(gates[:, :c])
    ig = jax.nn.sigmoid(gates[:, c:])
    lam = l_ref[...]
    softplus_neg = jnp.maximum(-lam, 0.0) + jnp.log1p(jnp.exp(-jnp.abs(lam)))
    log_a = (-LRU_C) * r * softplus_neg
    a = jnp.exp(log_a)
    b = jnp.sqrt(-jnp.tanh(log_a) * (a * a + 1.0)) * (ig * xc)
    a_scr[...] = a
    b_scr[...] = b

    rows = lax.broadcasted_iota(jnp.int32, (SUBLANES, c), 0)

    def tile(i, h):
        off = pl.multiple_of(i * SUBLANES, SUBLANES)
        at = a_scr[pl.ds(off, SUBLANES), :]
        bt = b_scr[pl.ds(off, SUBLANES), :]
        for d in (1, 2, 4):
            keep = rows >= d
            bt = jnp.where(keep, at * pltpu.roll(bt, d, axis=0) + bt, bt)
            at = jnp.where(keep, at * pltpu.roll(at, d, axis=0), at)
        ht = at * h + bt
        b_scr[pl.ds(off, SUBLANES), :] = ht
        return ht[SUBLANES - 1:SUBLANES, :]

    h_scr[...] = lax.fori_loop(0, tt // SUBLANES, tile, h_scr[...])
    y = b_scr[...] * _gelu(gb_ref[...])
    o_ref[...] = _rms(y, og_ref[...]).astype(o_ref.dtype)


def _rglru(xb, gb, conv_w, conv_b, w_gate_a, b_gate_a, w_gate_x, b_gate_x, lru_l, out_g, batch, seq):
    c = LRU_WIDTH
    eye = jnp.eye(LRU_BLOCKS, dtype=F32)
    dense = lambda w: jnp.einsum("nij,nm->nimj", w, eye).reshape(c, c)
    wg = jnp.concatenate([dense(w_gate_a), dense(w_gate_x)], axis=1).astype(BF16)
    bg = jnp.concatenate([b_gate_a.reshape(1, c), b_gate_x.reshape(1, c)], axis=1)
    blk = pl.BlockSpec((None, LRU_TILE, c), lambda b, j: (b, j, 0))
    out = pl.pallas_call(
        _rglru_kernel,
        out_shape=jax.ShapeDtypeStruct((batch, seq, c), BF16),
        grid=(batch, seq // LRU_TILE),
        in_specs=[blk, blk, _resident((CONV_WIDTH, c)), _resident((1, c)), _resident((c, 2 * c)),
                  _resident((1, 2 * c)), _resident((1, c)), _resident((1, c))],
        out_specs=blk,
        scratch_shapes=[pltpu.VMEM((LRU_TILE + SUBLANES, c), F32), pltpu.VMEM((LRU_TILE, c), F32),
                        pltpu.VMEM((LRU_TILE, c), F32), pltpu.VMEM((1, c), F32)],
        compiler_params=_params("parallel", "arbitrary"),
        name="rglru",
    )(xb.reshape(batch, seq, c), gb.reshape(batch, seq, c), conv_w, conv_b.reshape(1, c), wg, bg,
      lru_l.reshape(1, c), out_g.reshape(1, c))
    return out.reshape(batch * seq, c)


def _bias_kernel(rb_ref, onehot_ref, o_ref):
    o_ref[...] = jnp.dot(rb_ref[...], onehot_ref[...], preferred_element_type=F32,
                         precision=lax.Precision.HIGHEST)


def _t5_bucket(rel):
    n = jnp.maximum(rel, 0)
    max_exact = N_BUCKETS // 2
    nf = jnp.maximum(n, 1).astype(F32)
    large = max_exact + jnp.floor(jnp.log(nf / max_exact) / math.log(MAX_DISTANCE / max_exact)
                                  * (N_BUCKETS - max_exact)).astype(jnp.int32)
    large = jnp.minimum(large, N_BUCKETS - 1)
    return jnp.where(n < max_exact, n, large)


def _band_bias(rel_bias):
    i = jnp.arange(BLOCK)[:, None]
    j = jnp.arange(2 * BLOCK)[None, :]
    bucket = _t5_bucket(BLOCK + i - j).reshape(1, -1)
    onehot = (bucket == jnp.arange(N_BUCKETS)[:, None]).astype(F32)
    out = pl.pallas_call(
        _bias_kernel,
        out_shape=jax.ShapeDtypeStruct((N_HEADS, BLOCK * 2 * BLOCK), F32),
        name="band_bias",
    )(rel_bias.astype(F32).T, onehot)
    return out.reshape(N_HEADS, BLOCK, 2 * BLOCK)


def _swa_kernel(sink_ref, q_ref, kc_ref, kp_ref, vc_ref, vp_ref, bias_ref, qg_ref, kg_ref, og_ref, o_ref):
    kk = jnp.concatenate([kp_ref[...], kc_ref[...]], axis=0)
    vv = jnp.concatenate([vp_ref[...], vc_ref[...]], axis=0).astype(BF16)
    k_n = [_rms(kk[:, hk * HEAD_DIM:(hk + 1) * HEAD_DIM], kg_ref[...]).astype(BF16) for hk in range(N_KV_HEADS)]
    shape = (GROUP * BLOCK, 2 * BLOCK)
    qi = lax.broadcasted_iota(jnp.int32, shape, 0) % BLOCK
    kj = lax.broadcasted_iota(jnp.int32, shape, 1)
    chains = [(sub, hk) for sub in range(SWA_BLOCKS) for hk in range(N_KV_HEADS)]
    rows = lambda sub: slice(sub * BLOCK, (sub + 1) * BLOCK)
    keys = lambda sub: slice(sub * BLOCK, (sub + 2) * BLOCK)
    sink, s, m, p, denom, o = {}, {}, {}, {}, {}, {}
    for sub, hk in chains:
        first_key = jnp.where(pl.program_id(1) > 0, 0, BLOCK) if sub == 0 else 0
        valid = (kj > jnp.maximum(qi + (BLOCK - WINDOW), first_key - 1)) & (kj <= qi + BLOCK)
        heads = range(hk * GROUP, (hk + 1) * GROUP)
        q_n = jnp.concatenate([_rms(q_ref[rows(sub), h * HEAD_DIM:(h + 1) * HEAD_DIM], qg_ref[...]).astype(BF16)
                               for h in heads], axis=0)
        sink[sub, hk] = jnp.concatenate([jnp.full((BLOCK, 1), sink_ref[h], F32) for h in heads], axis=0)
        sc = lax.dot_general(q_n, k_n[hk][keys(sub)], (((1,), (1,)), ((), ())), preferred_element_type=F32)
        sc = sc * SCALE + bias_ref[hk * GROUP:(hk + 1) * GROUP].reshape(shape)
        s[sub, hk] = jnp.where(valid, sc, NEG_INF)
    for c in chains:
        m[c] = jnp.maximum(jnp.max(s[c], axis=-1, keepdims=True), sink[c])
    for c in chains:
        p[c] = jnp.exp(s[c] - m[c])
        denom[c] = jnp.sum(p[c], axis=-1, keepdims=True) + jnp.exp(sink[c] - m[c])
    for sub, hk in chains:
        o[sub, hk] = jnp.dot(p[sub, hk].astype(BF16), vv[keys(sub), hk * HEAD_DIM:(hk + 1) * HEAD_DIM],
                             preferred_element_type=F32) / denom[sub, hk]
    for sub in range(SWA_BLOCKS):
        y = jnp.concatenate([o[sub, hk][g * BLOCK:(g + 1) * BLOCK] for hk in range(N_KV_HEADS)
                             for g in range(GROUP)], axis=-1)
        o_ref[rows(sub), :] = _rms(y, og_ref[...]).astype(o_ref.dtype)


def _swa(q, k, v, bias, q_norm_g, k_norm_g, sinks, out_g, batch, seq):
    nb = seq // BLOCK
    cur = lambda w: pl.BlockSpec((None, SWA_BLOCKS * BLOCK, w), lambda b, n: (b, n, 0))
    prev = lambda w: pl.BlockSpec((None, BLOCK, w), lambda b, n: (b, jnp.maximum(SWA_BLOCKS * n - 1, 0), 0))
    k3 = k.reshape(batch, seq, KV_WIDTH)
    v3 = v.reshape(batch, seq, KV_WIDTH)
    out = pl.pallas_call(
        _swa_kernel,
        out_shape=jax.ShapeDtypeStruct((batch, seq, ATTN_WIDTH), BF16),
        grid=(batch, nb // SWA_BLOCKS),
        in_specs=[pl.BlockSpec(memory_space=pltpu.SMEM), cur(ATTN_WIDTH), cur(KV_WIDTH), prev(KV_WIDTH),
                  cur(KV_WIDTH), prev(KV_WIDTH), _resident((N_HEADS, BLOCK, 2 * BLOCK)),
                  _resident((1, HEAD_DIM)), _resident((1, HEAD_DIM)), _resident((1, ATTN_WIDTH))],
        out_specs=cur(ATTN_WIDTH),
        compiler_params=_params("parallel", "parallel"),
        name="swa",
    )(sinks.astype(F32), q.reshape(batch, seq, ATTN_WIDTH), k3, k3, v3, v3, bias,
      q_norm_g.reshape(1, HEAD_DIM), k_norm_g.reshape(1, HEAD_DIM), out_g.reshape(1, ATTN_WIDTH))
    return out.reshape(batch * seq, ATTN_WIDTH)


def _outproj_kernel(x_ref, ml_ref, ma_ref, w_ref, g_ref, x1_ref, xn_ref):
    acc = jnp.dot(ml_ref[...], w_ref[:LRU_WIDTH, :], preferred_element_type=F32)
    acc = acc + jnp.dot(ma_ref[...], w_ref[LRU_WIDTH:, :], preferred_element_type=F32)
    x1 = x_ref[...] + acc
    x1_ref[...] = x1
    xn_ref[...] = _rms(x1, g_ref[...])


def _outproj(x2, y_lru, y_att, w_out, g):
    t = x2.shape[0]
    row = lambda w: pl.BlockSpec((ROW_TILE, w), lambda i: (i, 0))
    return pl.pallas_call(
        _outproj_kernel,
        out_shape=[jax.ShapeDtypeStruct((t, D_MODEL), F32)] * 2,
        grid=(t // ROW_TILE,),
        in_specs=[row(D_MODEL), row(LRU_WIDTH), row(ATTN_WIDTH), _resident((D_MODEL, D_MODEL)),
                  _resident((1, D_MODEL))],
        out_specs=[row(D_MODEL)] * 2,
        compiler_params=_params("parallel"),
        name="outproj",
    )(x2, y_lru, y_att, w_out.astype(BF16), g.reshape(1, D_MODEL))


def _pair_candidates(s0, i0, s1, i1):
    vals, ids = [], []
    for a in range(TOPK):
        n_b = TOPK // (a + 1)
        vals.append(s0[a:a + 1] + s1[:n_b])
        ids.append(i0[a:a + 1] * N_KEYS + i1[:n_b])
    pad = (-sum(v.shape[0] for v in vals)) % SUBLANES
    if pad:
        vals.append(jnp.full((pad,) + s0.shape[1:], -jnp.inf, F32))
        ids.append(jnp.zeros((pad,) + s0.shape[1:], F32))
    return jnp.concatenate(vals, axis=0), jnp.concatenate(ids, axis=0)


def _top_rows(s, payload, k):
    n = s.shape[0]
    rows = lax.broadcasted_iota(jnp.int32, s.shape, 0).astype(F32)
    vals, picks = [], []
    for _ in range(k):
        m = jnp.max(s, axis=0, keepdims=True)
        first = jnp.min(jnp.where(s == m, rows, float(n)), axis=0, keepdims=True)
        hit = rows == first
        vals.append(m)
        if payload is None:
            picks.append(first)
        else:
            picks.append(jnp.max(jnp.where(hit, payload, -1.0), axis=0, keepdims=True))
        s = jnp.where(hit, -jnp.inf, s)
    return jnp.concatenate(vals, axis=0), jnp.concatenate(picks, axis=0)


def _peer_topk_kernel(xn_ref, wq_ref, keys_ref, off_ref, g_ref, q_scr, off_scr, g_scr):
    q_t = lax.dot_general(wq_ref[...], xn_ref[...].astype(BF16), (((1,), (1,)), ((), ())),
                          preferred_element_type=F32)
    q_scr[...] = q_t.astype(BF16)

    def head(h, carry):
        tops = []
        for half in range(2):
            hc = h * 2 + half
            q_hc = q_scr[pl.ds(pl.multiple_of(hc * D_HALF, D_HALF), D_HALF), :]
            s = jnp.dot(keys_ref[hc], q_hc, preferred_element_type=F32)
            tops.append(_top_rows(s, None, TOPK))
        (s0, i0), (s1, i1) = tops
        best, idx = _top_rows(*_pair_candidates(s0, i0, s1, i1), TOPK)
        e = jnp.exp(best - best[0:1])
        rows = pl.ds(pl.multiple_of(h * TOPK, TOPK), TOPK)
        off_scr[rows, :] = (idx * WORD_ROWS).astype(jnp.int32)
        g_scr[rows, :] = e / jnp.sum(e, axis=0, keepdims=True)
        return carry

    lax.fori_loop(0, PEER_HEADS, head, 0)
    off_ref[...] = off_scr[...].T
    g_ref[...] = g_scr[...].T


def _peer_topk(xn, w_query, sub_keys):
    t = xn.shape[0]
    nq = PEER_HEADS * D_QUERY
    out_blk = pl.BlockSpec((TOPK_TILE, N_PAIR), lambda i: (i, 0))
    return pl.pallas_call(
        _peer_topk_kernel,
        out_shape=[jax.ShapeDtypeStruct((t, N_PAIR), jnp.int32), jax.ShapeDtypeStruct((t, N_PAIR), F32)],
        grid=(t // TOPK_TILE,),
        in_specs=[pl.BlockSpec((TOPK_TILE, D_MODEL), lambda i: (i, 0)), _resident((nq, D_MODEL)),
                  _resident((2 * PEER_HEADS, N_KEYS, D_HALF))],
        out_specs=[out_blk, out_blk],
        scratch_shapes=[pltpu.VMEM((nq, TOPK_TILE), BF16), pltpu.VMEM((N_PAIR, TOPK_TILE), jnp.int32),
                        pltpu.VMEM((N_PAIR, TOPK_TILE), F32)],
        compiler_params=_params("parallel"),
        name="peer_topk",
    )(xn, w_query.T.astype(BF16), sub_keys.reshape(2 * PEER_HEADS, N_KEYS, D_HALF).astype(BF16))


def _pack_kernel(*refs):
    n_tables = len(refs) // 2
    for x_ref, o_ref in zip(refs[:n_tables], refs[n_tables:]):
        rows = x_ref.shape[0]
        for j in range(WORD_ROWS):
            lo = x_ref[:, j * LANES:(j + 1) * LANES]
            hi = x_ref[:, (WORD_ROWS + j) * LANES:(WORD_ROWS + j + 1) * LANES]
            o_ref[pl.ds(j, rows, stride=WORD_ROWS), :] = pltpu.pack_elementwise([lo, hi], packed_dtype=BF16)


def _pack_tables(*tabs):
    n = tabs[0].shape[0]
    return pl.pallas_call(
        _pack_kernel,
        out_shape=[jax.ShapeDtypeStruct((n * WORD_ROWS, LANES), PACKED)] * len(tabs),
        grid=(n // PACK_TILE,),
        in_specs=[pl.BlockSpec((PACK_TILE, D_MODEL), lambda i: (i, 0))] * len(tabs),
        out_specs=[pl.BlockSpec((PACK_TILE * WORD_ROWS, LANES), lambda i: (i, 0))] * len(tabs),
        compiler_params=_params("parallel"),
        name="pack_tables",
    )(*tabs)


def _slot_chunk(slot):
    return slot // 2 + WORD_ROWS * (slot % 2)


def _chunk_sum_matrix():
    return (np.arange(D_MODEL)[:, None] // N_CHUNK == np.arange(N_PAIR)[None, :]).astype(np.float32)


def _gathered_tile(off_ref, tab_ref, t, j):
    parts = [tab_ref[pl.ds(pl.multiple_of(off_ref[t, TILE_PAIRS * j + i], WORD_ROWS), WORD_ROWS), :]
             for i in range(TILE_PAIRS)]
    return pltpu.bitcast(jnp.concatenate(parts, axis=0), BF16)


def _select_dot(a, sel):
    out = None
    for _ in range(3):
        piece = a.astype(BF16)
        part = jnp.dot(piece, sel, preferred_element_type=F32)
        out = part if out is None else out + part
        a = a - piece.astype(F32)
    return out


def _swap_tile_rows(tiles):
    tiles = list(tiles)
    rows = lax.broadcasted_iota(jnp.int32, tiles[0].shape, 0)
    d = SUBLANES // 2
    while d:
        low = (rows & d) == 0
        out = list(tiles)
        for i in range(SUBLANES):
            if not i & d:
                a, b = tiles[i], tiles[i + d]
                out[i] = jnp.where(low, a, pltpu.roll(b, d, axis=0))
                out[i + d] = jnp.where(low, pltpu.roll(a, SUBLANES - d, axis=0), b)
        tiles = out
        d //= 2
    return tiles


def _split_bf16(a):
    hi = a.astype(BF16)
    return jnp.concatenate([hi, (a - hi.astype(F32)).astype(BF16)], axis=0)


def _for_each_token(off_now, off_next, off_bufs, sem, compute):
    s = pl.program_id(0)

    def fetch(src, which):
        return pltpu.make_async_copy(src.at[pl.ds(which * PEER_HALF, PEER_HALF)], off_bufs[which], sem.at[which])

    def consume(which):
        for t in range(PEER_HALF):
            compute(which * PEER_HALF + t, off_bufs[which], t)

    not_last = s + 1 < pl.num_programs(0)

    @pl.when(s == 0)
    def _():
        first = fetch(off_now, 0)
        first.start()
        first.wait()

    fetch(off_now, 1).start()
    consume(0)
    fetch(off_now, 1).wait()

    @pl.when(not_last)
    def _():
        fetch(off_next, 0).start()

    consume(1)

    @pl.when(not_last)
    def _():
        fetch(off_next, 0).wait()


def _group_row0():
    return pl.multiple_of((pl.program_id(0) % STEPS_PER_GROUP) * PEER_STEP, PEER_STEP)


def _peer_u_kernel(off_now, off_next, x_ref, tab_ref, sum_ref, gate_ref, w_ref, off_a, off_b, sem, z_scr):
    shape = (2 * N_CHUNK, MXU_ROWS)
    own = (lax.broadcasted_iota(jnp.int32, shape, 0) % N_CHUNK) == _slot_chunk(lax.broadcasted_iota(jnp.int32, shape, 1) % N_CHUNK)
    row0 = _group_row0()

    def compute(tok, off_ref, t):
        lhs = _split_bf16(x_ref[tok])
        zs = []
        for j in range(N_TILES):
            o = lax.dot_general(lhs, _gathered_tile(off_ref, tab_ref, t, j), (((1,), (1,)), ((), ())),
                                preferred_element_type=F32)
            zs.append(jnp.sum(jnp.where(own, o, 0.0), axis=0, keepdims=True))
        z_scr[pl.ds(row0 + tok, 1), :] = jnp.concatenate(zs, axis=1)

    _for_each_token(off_now, off_next, (off_a, off_b), sem, compute)

    @pl.when(pl.program_id(0) % STEPS_PER_GROUP == STEPS_PER_GROUP - 1)
    def _():
        act = _select_dot(z_scr[...], sum_ref[...])
        w_ref[...] = gate_ref[...] * _gelu(act)


def _peer_v_kernel(off_now, off_next, w_ref, x1_ref, tab_ref, spread_ref, y_ref, off_a, off_b, sem, wx_scr):
    shape = (N_CHUNK, D_MODEL)
    own = lax.broadcasted_iota(jnp.int32, shape, 0) == _slot_chunk(lax.broadcasted_iota(jnp.int32, shape, 1) % N_CHUNK)
    row0 = _group_row0()

    @pl.when(pl.program_id(0) % STEPS_PER_GROUP == 0)
    def _():
        wx_scr[...] = _select_dot(w_ref[...], spread_ref[...])

    def compute(tok, off_ref, t):
        lhs = _split_bf16(jnp.where(own, jnp.broadcast_to(wx_scr[pl.ds(row0 + tok, 1), :], shape), 0.0))
        r = None
        for j in range(N_TILES):
            part = jnp.dot(lhs[:, j * MXU_ROWS:(j + 1) * MXU_ROWS], _gathered_tile(off_ref, tab_ref, t, j),
                           preferred_element_type=F32)
            r = part if r is None else r + part
        pending.append(r[:N_CHUNK] + r[N_CHUNK:])
        if len(pending) == SUBLANES:
            by_chunk = _swap_tile_rows(pending)
            pending.clear()
            toks = slice(tok + 1 - SUBLANES, tok + 1)
            for c in range(N_CHUNK):
                cols = slice(c * LANES, (c + 1) * LANES)
                y_ref[toks, cols] = x1_ref[toks, cols] + by_chunk[c]

    pending = []
    _for_each_token(off_now, off_next, (off_a, off_b), sem, compute)


def _peer_specs(n_steps):
    pairs = pl.BlockSpec((PEER_GROUP, N_PAIR), lambda i: (i // STEPS_PER_GROUP, 0))
    rows = pl.BlockSpec((PEER_STEP, N_CHUNK, LANES), lambda i: (i, 0, 0))
    table = _resident((N_EXPERTS * WORD_ROWS, LANES))
    scratch = ([pltpu.SMEM((PEER_HALF, N_PAIR), jnp.int32)] * 2 + [pltpu.SemaphoreType.DMA((2,))]
               + [pltpu.VMEM((PEER_GROUP, D_MODEL), F32)])
    offs = [pl.BlockSpec((PEER_STEP, N_PAIR), lambda i: (i, 0)),
            pl.BlockSpec((PEER_STEP, N_PAIR), lambda i: (jnp.minimum(i + 1, n_steps - 1), 0))]
    return offs, pairs, rows, table, scratch


def _peer_u(off, gate, xn, tab_u):
    t = off.shape[0]
    offs, pairs, rows, table, scratch = _peer_specs(t // PEER_STEP)
    return pl.pallas_call(
        _peer_u_kernel,
        out_shape=jax.ShapeDtypeStruct((t, N_PAIR), F32),
        grid=(t // PEER_STEP,),
        in_specs=offs + [rows, table, _resident((D_MODEL, N_PAIR)), pairs],
        out_specs=pairs,
        scratch_shapes=scratch,
        compiler_params=_params("arbitrary"),
        name="peer_u",
    )(off, off, xn.reshape(t, N_CHUNK, LANES), tab_u, jnp.asarray(_chunk_sum_matrix(), BF16), gate)


def _peer_v(off, w, x1, tab_v):
    t = off.shape[0]
    offs, pairs, rows, table, scratch = _peer_specs(t // PEER_STEP)
    flat = pl.BlockSpec((PEER_STEP, D_MODEL), lambda i: (i, 0))
    return pl.pallas_call(
        _peer_v_kernel,
        out_shape=jax.ShapeDtypeStruct((t, D_MODEL), F32),
        grid=(t // PEER_STEP,),
        in_specs=offs + [pairs, flat, table, _resident((N_PAIR, D_MODEL))],
        out_specs=flat,
        scratch_shapes=scratch,
        compiler_params=_params("arbitrary"),
        name="peer_v",
    )(off, off, w, x1, tab_v, jnp.asarray(_chunk_sum_matrix().T, BF16))


def kernel(x, ln_mix_g, w_in, conv_w, conv_b, w_gate_a, b_gate_a, w_gate_x, b_gate_x, lru_L, q_norm_g,
           k_norm_g, sinks, lru_out_g, attn_out_g, w_out, ln_ffn_g, w_query, sub_keys, expert_u, expert_v,
           rel_bias):
    batch, seq, _ = x.shape
    bias = _band_bias(rel_bias)
    x2 = x.reshape(batch * seq, D_MODEL)
    for l in range(w_in.shape[0]):
        xb, gb, q, k, v = _inproj(x2, ln_mix_g[l], w_in[l])
        y_lru = _rglru(xb, gb, conv_w[l], conv_b[l], w_gate_a[l], b_gate_a[l], w_gate_x[l], b_gate_x[l],
                       lru_L[l], lru_out_g[l], batch, seq)
        y_att = _swa(q, k, v, bias, q_norm_g[l], k_norm_g[l], sinks[l], attn_out_g[l], batch, seq)
        x1, xn = _outproj(x2, y_lru, y_att, w_out[l], ln_ffn_g[l])
        off, gate = _peer_topk(xn, w_query[l], sub_keys[l])
        tab_u, tab_v = _pack_tables(expert_u[l], expert_v[l])
        w = _peer_u(off, gate, xn, tab_u)
        x2 = _peer_v(off, w, x1, tab_v)
    return x2.reshape(batch, seq, D_MODEL)
```

```python
import math

import jax
import jax.numpy as jnp
import numpy as np
from jax import lax
from jax.experimental import pallas as pl
from jax.experimental.pallas import tpu as pltpu

D_MODEL = 1024
LRU_WIDTH = 512
LRU_BLOCKS = 8
LRU_BLOCK = LRU_WIDTH // LRU_BLOCKS
CONV_WIDTH = 4
LRU_C = 8.0
N_HEADS = 8
N_KV_HEADS = 2
GROUP = N_HEADS // N_KV_HEADS
HEAD_DIM = 64
ATTN_WIDTH = N_HEADS * HEAD_DIM
KV_WIDTH = N_KV_HEADS * HEAD_DIM
WINDOW = 128
BLOCK = 128
N_BUCKETS = 32
MAX_DISTANCE = 128
PEER_HEADS = 8
N_KEYS = 128
N_EXPERTS = N_KEYS * N_KEYS
D_QUERY = 256
D_HALF = D_QUERY // 2
TOPK = 16
N_PAIR = PEER_HEADS * TOPK
IN_COLS = 2 * LRU_WIDTH + ATTN_WIDTH + 2 * KV_WIDTH
EPS = 1e-6
NEG_INF = -1e30
SCALE = HEAD_DIM ** -0.5

F32 = jnp.float32
BF16 = jnp.bfloat16
PACKED = jnp.uint32
LANES = 128
SUBLANES = 8
N_CHUNK = D_MODEL // LANES
WORD_ROWS = N_CHUNK // 2
MXU_ROWS = 256
TILE_PAIRS = MXU_ROWS // N_CHUNK
N_TILES = N_PAIR // TILE_PAIRS
VMEM_LIMIT = 52 * 1024 * 1024

ROW_TILE = 1024
PACK_TILE = 1024
LRU_TILE = 512
SWA_BLOCKS = 4
TOPK_TILE = 1024
PEER_HALF = 64
PEER_STEP = 2 * PEER_HALF
PEER_GROUP = 256
STEPS_PER_GROUP = PEER_GROUP // PEER_STEP


def _rms(x, g):
    return x * lax.rsqrt(jnp.mean(x * x, axis=-1, keepdims=True) + EPS) * g


def _gelu(x):
    return 0.5 * x * (1.0 + jnp.tanh(math.sqrt(2.0 / math.pi) * (x + 0.044715 * (x * x * x))))


def _params(*sem):
    return pltpu.CompilerParams(dimension_semantics=sem, vmem_limit_bytes=VMEM_LIMIT)


def _resident(shape):
    zeros = (0,) * len(shape)
    return pl.BlockSpec(shape, lambda *_: zeros, pipeline_mode=pl.Buffered(1))


def _inproj_kernel(x_ref, g_ref, w_ref, xb_ref, gb_ref, q_ref, k_ref, v_ref):
    h = _rms(x_ref[...], g_ref[...])
    p = jnp.dot(h.astype(BF16), w_ref[...], preferred_element_type=F32)
    c0, c1, c2, c3 = LRU_WIDTH, 2 * LRU_WIDTH, 2 * LRU_WIDTH + ATTN_WIDTH, IN_COLS - KV_WIDTH
    xb_ref[...] = p[:, :c0]
    gb_ref[...] = p[:, c0:c1]
    q_ref[...] = p[:, c1:c2]
    k_ref[...] = p[:, c2:c3]
    v_ref[...] = p[:, c3:]


def _inproj(x2, g, w_in):
    t = x2.shape[0]
    widths = (LRU_WIDTH, LRU_WIDTH, ATTN_WIDTH, KV_WIDTH, KV_WIDTH)
    row = lambda w: pl.BlockSpec((ROW_TILE, w), lambda i: (i, 0))
    return pl.pallas_call(
        _inproj_kernel,
        out_shape=[jax.ShapeDtypeStruct((t, w), F32) for w in widths],
        grid=(t // ROW_TILE,),
        in_specs=[row(D_MODEL), _resident((1, D_MODEL)), _resident((D_MODEL, IN_COLS))],
        out_specs=[row(w) for w in widths],
        compiler_params=_params("parallel"),
        name="inproj",
    )(x2, g.reshape(1, D_MODEL), w_in.astype(BF16))


def _rglru_kernel(xb_ref, gb_ref, cw_ref, cb_ref, wg_ref, bg_ref, l_ref, og_ref, o_ref,
                  xs_scr, a_scr, b_scr, h_scr):
    tt, c = xb_ref.shape

    @pl.when(pl.program_id(1) == 0)
    def _():
        xs_scr[0:SUBLANES, :] = jnp.zeros((SUBLANES, c), F32)
        h_scr[...] = jnp.zeros_like(h_scr)

    xb = xb_ref[...]
    xs_scr[SUBLANES:SUBLANES + tt, :] = xb
    xc = cb_ref[...] + xb * cw_ref[CONV_WIDTH - 1:CONV_WIDTH, :]
    for back in range(1, CONV_WIDTH):
        tap = CONV_WIDTH - 1 - back
        xc = xc + xs_scr[SUBLANES - back:SUBLANES - back + tt, :] * cw_ref[tap:tap + 1, :]
    xs_scr[0:SUBLANES, :] = xb[tt - SUBLANES:, :]

    gates = jnp.dot(xc.astype(BF16), wg_ref[...], preferred_element_type=F32) + bg_ref[...]
    r = jax.nn.sigmoid(gates[:, :c])
    ig = jax.nn.sigmoid(gates[:, c:])
    lam = l_ref[...]
    softplus_neg = jnp.maximum(-lam, 0.0) + jnp.log1p(jnp.exp(-jnp.abs(lam)))
    log_a = (-LRU_C) * r * softplus_neg
    a = jnp.exp(log_a)
    b = jnp.sqrt(-jnp.tanh(log_a) * (a * a + 1.0)) * (ig * xc)
    a_scr[...] = a
    b_scr[...] = b

    rows = lax.broadcasted_iota(jnp.int32, (SUBLANES, c), 0)

    def tile(i, h):
        off = pl.multiple_of(i * SUBLANES, SUBLANES)
        at = a_scr[pl.ds(off, SUBLANES), :]
        bt = b_scr[pl.ds(off, SUBLANES), :]
        for d in (1, 2, 4):
            keep = rows >= d
            bt = jnp.where(keep, at * pltpu.roll(bt, d, axis=0) + bt, bt)
            at = jnp.where(keep, at * pltpu.roll(at, d, axis=0), at)
        ht = at * h + bt
        b_scr[pl.ds(off, SUBLANES), :] = ht
        return ht[SUBLANES - 1:SUBLANES, :]

    h_scr[...] = lax.fori_loop(0, tt // SUBLANES, tile, h_scr[...])
    y = b_scr[...] * _gelu(gb_ref[...])
    o_ref[...] = _rms(y, og_ref[...]).astype(o_ref.dtype)


def _rglru(xb, gb, conv_w, conv_b, w_gate_a, b_gate_a, w_gate_x, b_gate_x, lru_l, out_g, batch, seq):
    c = LRU_WIDTH
    eye = jnp.eye(LRU_BLOCKS, dtype=F32)
    dense = lambda w: jnp.einsum("nij,nm->nimj", w, eye).reshape(c, c)
    wg = jnp.concatenate([dense(w_gate_a), dense(w_gate_x)], axis=1).astype(BF16)
    bg = jnp.concatenate([b_gate_a.reshape(1, c), b_gate_x.reshape(1, c)], axis=1)
    blk = pl.BlockSpec((None, LRU_TILE, c), lambda b, j: (b, j, 0))
    out = pl.pallas_call(
        _rglru_kernel,
        out_shape=jax.ShapeDtypeStruct((batch, seq, c), BF16),
        grid=(batch, seq // LRU_TILE),
        in_specs=[blk, blk, _resident((CONV_WIDTH, c)), _resident((1, c)), _resident((c, 2 * c)),
                  _resident((1, 2 * c)), _resident((1, c)), _resident((1, c))],
        out_specs=blk,
        scratch_shapes=[pltpu.VMEM((LRU_TILE + SUBLANES, c), F32), pltpu.VMEM((LRU_TILE, c), F32),
                        pltpu.VMEM((LRU_TILE, c), F32), pltpu.VMEM((1, c), F32)],
        compiler_params=_params("parallel", "arbitrary"),
        name="rglru",
    )(xb.reshape(batch, seq, c), gb.reshape(batch, seq, c), conv_w, conv_b.reshape(1, c), wg, bg,
      lru_l.reshape(1, c), out_g.reshape(1, c))
    return out.reshape(batch * seq, c)


def _bias_kernel(rb_ref, onehot_ref, o_ref):
    o_ref[...] = jnp.dot(rb_ref[...], onehot_ref[...], preferred_element_type=F32,
                         precision=lax.Precision.HIGHEST)


def _t5_bucket(rel):
    n = jnp.maximum(rel, 0)
    max_exact = N_BUCKETS // 2
    nf = jnp.maximum(n, 1).astype(F32)
    large = max_exact + jnp.floor(jnp.log(nf / max_exact) / math.log(MAX_DISTANCE / max_exact)
                                  * (N_BUCKETS - max_exact)).astype(jnp.int32)
    large = jnp.minimum(large, N_BUCKETS - 1)
    return jnp.where(n < max_exact, n, large)


def _band_bias(rel_bias):
    i = jnp.arange(BLOCK)[:, None]
    j = jnp.arange(2 * BLOCK)[None, :]
    bucket = _t5_bucket(BLOCK + i - j).reshape(1, -1)
    onehot = (bucket == jnp.arange(N_BUCKETS)[:, None]).astype(F32)
    out = pl.pallas_call(
        _bias_kernel,
        out_shape=jax.ShapeDtypeStruct((N_HEADS, BLOCK * 2 * BLOCK), F32),
        name="band_bias",
    )(rel_bias.astype(F32).T, onehot)
    return out.reshape(N_HEADS, BLOCK, 2 * BLOCK)


def _swa_kernel(sink_ref, q_ref, kc_ref, kp_ref, vc_ref, vp_ref, bias_ref, qg_ref, kg_ref, og_ref, o_ref):
    kk = jnp.concatenate([kp_ref[...], kc_ref[...]], axis=0)
    vv = jnp.concatenate([vp_ref[...], vc_ref[...]], axis=0).astype(BF16)
    k_n = [_rms(kk[:, hk * HEAD_DIM:(hk + 1) * HEAD_DIM], kg_ref[...]).astype(BF16) for hk in range(N_KV_HEADS)]
    shape = (GROUP * BLOCK, 2 * BLOCK)
    qi = lax.broadcasted_iota(jnp.int32, shape, 0) % BLOCK
    kj = lax.broadcasted_iota(jnp.int32, shape, 1)
    chains = [(sub, hk) for sub in range(SWA_BLOCKS) for hk in range(N_KV_HEADS)]
    rows = lambda sub: slice(sub * BLOCK, (sub + 1) * BLOCK)
    keys = lambda sub: slice(sub * BLOCK, (sub + 2) * BLOCK)
    sink, s, m, p, denom, o = {}, {}, {}, {}, {}, {}
    for sub, hk in chains:
        first_key = jnp.where(pl.program_id(1) > 0, 0, BLOCK) if sub == 0 else 0
        valid = (kj > jnp.maximum(qi + (BLOCK - WINDOW), first_key - 1)) & (kj <= qi + BLOCK)
        heads = range(hk * GROUP, (hk + 1) * GROUP)
        q_n = jnp.concatenate([_rms(q_ref[rows(sub), h * HEAD_DIM:(h + 1) * HEAD_DIM], qg_ref[...]).astype(BF16)
                               for h in heads], axis=0)
        sink[sub, hk] = jnp.concatenate([jnp.full((BLOCK, 1), sink_ref[h], F32) for h in heads], axis=0)
        sc = lax.dot_general(q_n, k_n[hk][keys(sub)], (((1,), (1,)), ((), ())), preferred_element_type=F32)
        sc = sc * SCALE + bias_ref[hk * GROUP:(hk + 1) * GROUP].reshape(shape)
        s[sub, hk] = jnp.where(valid, sc, NEG_INF)
    for c in chains:
        m[c] = jnp.maximum(jnp.max(s[c], axis=-1, keepdims=True), sink[c])
    for c in chains:
        p[c] = jnp.exp(s[c] - m[c])
        denom[c] = jnp.sum(p[c], axis=-1, keepdims=True) + jnp.exp(sink[c] - m[c])
    for sub, hk in chains:
        o[sub, hk] = jnp.dot(p[sub, hk].astype(BF16), vv[keys(sub), hk * HEAD_DIM:(hk + 1) * HEAD_DIM],
                             preferred_element_type=F32) / denom[sub, hk]
    for sub in range(SWA_BLOCKS):
        y = jnp.concatenate([o[sub, hk][g * BLOCK:(g + 1) * BLOCK] for hk in range(N_KV_HEADS)
                             for g in range(GROUP)], axis=-1)
        o_ref[rows(sub), :] = _rms(y, og_ref[...]).astype(o_ref.dtype)


def _swa(q, k, v, bias, q_norm_g, k_norm_g, sinks, out_g, batch, seq):
    nb = seq // BLOCK
    cur = lambda w: pl.BlockSpec((None, SWA_BLOCKS * BLOCK, w), lambda b, n: (b, n, 0))
    prev = lambda w: pl.BlockSpec((None, BLOCK, w), lambda b, n: (b, jnp.maximum(SWA_BLOCKS * n - 1, 0), 0))
    k3 = k.reshape(batch, seq, KV_WIDTH)
    v3 = v.reshape(batch, seq, KV_WIDTH)
    out = pl.pallas_call(
        _swa_kernel,
        out_shape=jax.ShapeDtypeStruct((batch, seq, ATTN_WIDTH), BF16),
        grid=(batch, nb // SWA_BLOCKS),
        in_specs=[pl.BlockSpec(memory_space=pltpu.SMEM), cur(ATTN_WIDTH), cur(KV_WIDTH), prev(KV_WIDTH),
                  cur(KV_WIDTH), prev(KV_WIDTH), _resident((N_HEADS, BLOCK, 2 * BLOCK)),
                  _resident((1, HEAD_DIM)), _resident((1, HEAD_DIM)), _resident((1, ATTN_WIDTH))],
        out_specs=cur(ATTN_WIDTH),
        compiler_params=_params("parallel", "parallel"),
        name="swa",
    )(sinks.astype(F32), q.reshape(batch, seq, ATTN_WIDTH), k3, k3, v3, v3, bias,
      q_norm_g.reshape(1, HEAD_DIM), k_norm_g.reshape(1, HEAD_DIM), out_g.reshape(1, ATTN_WIDTH))
    return out.reshape(batch * seq, ATTN_WIDTH)


def _outproj_kernel(x_ref, ml_ref, ma_ref, w_ref, g_ref, x1_ref, xn_ref):
    acc = jnp.dot(ml_ref[...], w_ref[:LRU_WIDTH, :], preferred_element_type=F32)
    acc = acc + jnp.dot(ma_ref[...], w_ref[LRU_WIDTH:, :], preferred_element_type=F32)
    x1 = x_ref[...] + acc
    x1_ref[...] = x1
    xn_ref[...] = _rms(x1, g_ref[...])


def _outproj(x2, y_lru, y_att, w_out, g):
    t = x2.shape[0]
    row = lambda w: pl.BlockSpec((ROW_TILE, w), lambda i: (i, 0))
    return pl.pallas_call(
        _outproj_kernel,
        out_shape=[jax.ShapeDtypeStruct((t, D_MODEL), F32)] * 2,
        grid=(t // ROW_TILE,),
        in_specs=[row(D_MODEL), row(LRU_WIDTH), row(ATTN_WIDTH), _resident((D_MODEL, D_MODEL)),
                  _resident((1, D_MODEL))],
        out_specs=[row(D_MODEL)] * 2,
        compiler_params=_params("parallel"),
        name="outproj",
    )(x2, y_lru, y_att, w_out.astype(BF16), g.reshape(1, D_MODEL))


def _pair_candidates(s0, i0, s1, i1):
    vals, ids = [], []
    for a in range(TOPK):
        n_b = TOPK // (a + 1)
        vals.append(s0[a:a + 1] + s1[:n_b])
        ids.append(i0[a:a + 1] * N_KEYS + i1[:n_b])
    pad = (-sum(v.shape[0] for v in vals)) % SUBLANES
    if pad:
        vals.append(jnp.full((pad,) + s0.shape[1:], -jnp.inf, F32))
        ids.append(jnp.zeros((pad,) + s0.shape[1:], F32))
    return jnp.concatenate(vals, axis=0), jnp.concatenate(ids, axis=0)


def _top_rows(s, payload, k):
    n = s.shape[0]
    rows = lax.broadcasted_iota(jnp.int32, s.shape, 0).astype(F32)
    vals, picks = [], []
    for _ in range(k):
        m = jnp.max(s, axis=0, keepdims=True)
        first = jnp.min(jnp.where(s == m, rows, float(n)), axis=0, keepdims=True)
        hit = rows == first
        vals.append(m)
        if payload is None:
            picks.append(first)
        else:
            picks.append(jnp.max(jnp.where(hit, payload, -1.0), axis=0, keepdims=True))
        s = jnp.where(hit, -jnp.inf, s)
    return jnp.concatenate(vals, axis=0), jnp.concatenate(picks, axis=0)


def _peer_topk_kernel(xn_ref, wq_ref, keys_ref, off_ref, g_ref, q_scr, off_scr, g_scr):
    q_t = lax.dot_general(wq_ref[...], xn_ref[...].astype(BF16), (((1,), (1,)), ((), ())),
                          preferred_element_type=F32)
    q_scr[...] = q_t.astype(BF16)

    def head(h, carry):
        scores = []
        for half in range(2):
            hc = h * 2 + half
            q_hc = q_scr[pl.ds(pl.multiple_of(hc * D_HALF, D_HALF), D_HALF), :]
            scores.append(jnp.dot(keys_ref[hc], q_hc, preferred_element_type=F32))
        n_tok = scores[0].shape[1]
        tops, ids = _top_rows(jnp.concatenate(scores, axis=1), None, TOPK)
        s0, s1, i0, i1 = tops[:, :n_tok], tops[:, n_tok:], ids[:, :n_tok], ids[:, n_tok:]
        best, idx = _top_rows(*_pair_candidates(s0, i0, s1, i1), TOPK)
        e = jnp.exp(best - best[0:1])
        rows = pl.ds(pl.multiple_of(h * TOPK, TOPK), TOPK)
        off_scr[rows, :] = (idx * WORD_ROWS).astype(jnp.int32)
        g_scr[rows, :] = e / jnp.sum(e, axis=0, keepdims=True)
        return carry

    lax.fori_loop(0, PEER_HEADS, head, 0)
    off_ref[...] = off_scr[...].T
    g_ref[...] = g_scr[...].T


def _peer_topk(xn, w_query, sub_keys):
    t = xn.shape[0]
    nq = PEER_HEADS * D_QUERY
    out_blk = pl.BlockSpec((TOPK_TILE, N_PAIR), lambda i: (i, 0))
    return pl.pallas_call(
        _peer_topk_kernel,
        out_shape=[jax.ShapeDtypeStruct((t, N_PAIR), jnp.int32), jax.ShapeDtypeStruct((t, N_PAIR), F32)],
        grid=(t // TOPK_TILE,),
        in_specs=[pl.BlockSpec((TOPK_TILE, D_MODEL), lambda i: (i, 0)), _resident((nq, D_MODEL)),
                  _resident((2 * PEER_HEADS, N_KEYS, D_HALF))],
        out_specs=[out_blk, out_blk],
        scratch_shapes=[pltpu.VMEM((nq, TOPK_TILE), BF16), pltpu.VMEM((N_PAIR, TOPK_TILE), jnp.int32),
                        pltpu.VMEM((N_PAIR, TOPK_TILE), F32)],
        compiler_params=_params("parallel"),
        name="peer_topk",
    )(xn, w_query.T.astype(BF16), sub_keys.reshape(2 * PEER_HEADS, N_KEYS, D_HALF).astype(BF16))


def _pack_kernel(*refs):
    n_tables = len(refs) // 2
    for x_ref, o_ref in zip(refs[:n_tables], refs[n_tables:]):
        rows = x_ref.shape[0]
        for j in range(WORD_ROWS):
            lo = x_ref[:, j * LANES:(j + 1) * LANES]
            hi = x_ref[:, (WORD_ROWS + j) * LANES:(WORD_ROWS + j + 1) * LANES]
            o_ref[pl.ds(j, rows, stride=WORD_ROWS), :] = pltpu.pack_elementwise([lo, hi], packed_dtype=BF16)


def _pack_tables(*tabs):
    n = tabs[0].shape[0]
    return pl.pallas_call(
        _pack_kernel,
        out_shape=[jax.ShapeDtypeStruct((n * WORD_ROWS, LANES), PACKED)] * len(tabs),
        grid=(n // PACK_TILE,),
        in_specs=[pl.BlockSpec((PACK_TILE, D_MODEL), lambda i: (i, 0))] * len(tabs),
        out_specs=[pl.BlockSpec((PACK_TILE * WORD_ROWS, LANES), lambda i: (i, 0))] * len(tabs),
        compiler_params=_params("parallel"),
        name="pack_tables",
    )(*tabs)


def _slot_chunk(slot):
    return slot // 2 + WORD_ROWS * (slot % 2)


def _chunk_sum_matrix():
    return (np.arange(D_MODEL)[:, None] // N_CHUNK == np.arange(N_PAIR)[None, :]).astype(np.float32)


def _gathered_tile(off_ref, tab_ref, t, j):
    parts = [tab_ref[pl.ds(pl.multiple_of(off_ref[t, TILE_PAIRS * j + i], WORD_ROWS), WORD_ROWS), :]
             for i in range(TILE_PAIRS)]
    return pltpu.bitcast(jnp.concatenate(parts, axis=0), BF16)


def _select_dot(a, sel):
    out = None
    for _ in range(3):
        piece = a.astype(BF16)
        part = jnp.dot(piece, sel, preferred_element_type=F32)
        out = part if out is None else out + part
        a = a - piece.astype(F32)
    return out


def _swap_tile_rows(tiles):
    tiles = list(tiles)
    rows = lax.broadcasted_iota(jnp.int32, tiles[0].shape, 0)
    d = SUBLANES // 2
    while d:
        low = (rows & d) == 0
        out = list(tiles)
        for i in range(SUBLANES):
            if not i & d:
                a, b = tiles[i], tiles[i + d]
                out[i] = jnp.where(low, a, pltpu.roll(b, d, axis=0))
                out[i + d] = jnp.where(low, pltpu.roll(a, SUBLANES - d, axis=0), b)
        tiles = out
        d //= 2
    return tiles


def _split_bf16(a):
    hi = a.astype(BF16)
    return jnp.concatenate([hi, (a - hi.astype(F32)).astype(BF16)], axis=0)


def _for_each_token(off_now, off_next, off_bufs, sem, compute):
    s = pl.program_id(0)

    def fetch(src, which):
        return pltpu.make_async_copy(src.at[pl.ds(which * PEER_HALF, PEER_HALF)], off_bufs[which], sem.at[which])

    def consume(which):
        for t in range(PEER_HALF):
            compute(which * PEER_HALF + t, off_bufs[which], t)

    not_last = s + 1 < pl.num_programs(0)

    @pl.when(s == 0)
    def _():
        first = fetch(off_now, 0)
        first.start()
        first.wait()

    fetch(off_now, 1).start()
    consume(0)
    fetch(off_now, 1).wait()

    @pl.when(not_last)
    def _():
        fetch(off_next, 0).start()

    consume(1)

    @pl.when(not_last)
    def _():
        fetch(off_next, 0).wait()


def _group_row0():
    return pl.multiple_of((pl.program_id(0) % STEPS_PER_GROUP) * PEER_STEP, PEER_STEP)


def _peer_u_kernel(off_now, off_next, x_ref, tab_ref, sum_ref, gate_ref, w_ref, off_a, off_b, sem, z_scr):
    shape = (2 * N_CHUNK, MXU_ROWS)
    own = (lax.broadcasted_iota(jnp.int32, shape, 0) % N_CHUNK) == _slot_chunk(lax.broadcasted_iota(jnp.int32, shape, 1) % N_CHUNK)
    row0 = _group_row0()

    def compute(tok, off_ref, t):
        lhs = _split_bf16(x_ref[tok])
        zs = []
        for j in range(N_TILES):
            o = lax.dot_general(lhs, _gathered_tile(off_ref, tab_ref, t, j), (((1,), (1,)), ((), ())),
                                preferred_element_type=F32)
            zs.append(jnp.sum(jnp.where(own, o, 0.0), axis=0, keepdims=True))
        z_scr[pl.ds(row0 + tok, 1), :] = jnp.concatenate(zs, axis=1)

    _for_each_token(off_now, off_next, (off_a, off_b), sem, compute)

    @pl.when(pl.program_id(0) % STEPS_PER_GROUP == STEPS_PER_GROUP - 1)
    def _():
        act = _select_dot(z_scr[...], sum_ref[...])
        w_ref[...] = gate_ref[...] * _gelu(act)


def _peer_v_kernel(off_now, off_next, w_ref, x1_ref, tab_ref, spread_ref, y_ref, off_a, off_b, sem, wx_scr):
    shape = (N_CHUNK, D_MODEL)
    own = lax.broadcasted_iota(jnp.int32, shape, 0) == _slot_chunk(lax.broadcasted_iota(jnp.int32, shape, 1) % N_CHUNK)
    row0 = _group_row0()

    @pl.when(pl.program_id(0) % STEPS_PER_GROUP == 0)
    def _():
        wx_scr[...] = _select_dot(w_ref[...], spread_ref[...])

    def compute(tok, off_ref, t):
        lhs = _split_bf16(jnp.where(own, jnp.broadcast_to(wx_scr[pl.ds(row0 + tok, 1), :], shape), 0.0))
        r = None
        for j in range(N_TILES):
            part = jnp.dot(lhs[:, j * MXU_ROWS:(j + 1) * MXU_ROWS], _gathered_tile(off_ref, tab_ref, t, j),
                           preferred_element_type=F32)
            r = part if r is None else r + part
        pending.append(r[:N_CHUNK] + r[N_CHUNK:])
        if len(pending) == SUBLANES:
            by_chunk = _swap_tile_rows(pending)
            pending.clear()
            toks = slice(tok + 1 - SUBLANES, tok + 1)
            for c in range(N_CHUNK):
                cols = slice(c * LANES, (c + 1) * LANES)
                y_ref[toks, cols] = x1_ref[toks, cols] + by_chunk[c]

    pending = []
    _for_each_token(off_now, off_next, (off_a, off_b), sem, compute)


def _peer_specs(n_steps):
    pairs = pl.BlockSpec((PEER_GROUP, N_PAIR), lambda i: (i // STEPS_PER_GROUP, 0))
    rows = pl.BlockSpec((PEER_STEP, N_CHUNK, LANES), lambda i: (i, 0, 0))
    table = _resident((N_EXPERTS * WORD_ROWS, LANES))
    scratch = ([pltpu.SMEM((PEER_HALF, N_PAIR), jnp.int32)] * 2 + [pltpu.SemaphoreType.DMA((2,))]
               + [pltpu.VMEM((PEER_GROUP, D_MODEL), F32)])
    offs = [pl.BlockSpec((PEER_STEP, N_PAIR), lambda i: (i, 0)),
            pl.BlockSpec((PEER_STEP, N_PAIR), lambda i: (jnp.minimum(i + 1, n_steps - 1), 0))]
    return offs, pairs, rows, table, scratch


def _peer_u(off, gate, xn, tab_u):
    t = off.shape[0]
    offs, pairs, rows, table, scratch = _peer_specs(t // PEER_STEP)
    return pl.pallas_call(
        _peer_u_kernel,
        out_shape=jax.ShapeDtypeStruct((t, N_PAIR), F32),
        grid=(t // PEER_STEP,),
        in_specs=offs + [rows, table, _resident((D_MODEL, N_PAIR)), pairs],
        out_specs=pairs,
        scratch_shapes=scratch,
        compiler_params=_params("arbitrary"),
        name="peer_u",
    )(off, off, xn.reshape(t, N_CHUNK, LANES), tab_u, jnp.asarray(_chunk_sum_matrix(), BF16), gate)


def _peer_v(off, w, x1, tab_v):
    t = off.shape[0]
    offs, pairs, rows, table, scratch = _peer_specs(t // PEER_STEP)
    flat = pl.BlockSpec((PEER_STEP, D_MODEL), lambda i: (i, 0))
    return pl.pallas_call(
        _peer_v_kernel,
        out_shape=jax.ShapeDtypeStruct((t, D_MODEL), F32),
        grid=(t // PEER_STEP,),
        in_specs=offs + [pairs, flat, table, _resident((N_PAIR, D_MODEL))],
        out_specs=flat,
        scratch_shapes=scratch,
        compiler_params=_params("arbitrary"),
        name="peer_v",
    )(off, off, w, x1, tab_v, jnp.asarray(_chunk_sum_matrix().T, BF16))


def kernel(x, ln_mix_g, w_in, conv_w, conv_b, w_gate_a, b_gate_a, w_gate_x, b_gate_x, lru_L, q_norm_g,
           k_norm_g, sinks, lru_out_g, attn_out_g, w_out, ln_ffn_g, w_query, sub_keys, expert_u, expert_v,
           rel_bias):
    batch, seq, _ = x.shape
    bias = _band_bias(rel_bias)
    x2 = x.reshape(batch * seq, D_MODEL)
    for l in range(w_in.shape[0]):
        xb, gb, q, k, v = _inproj(x2, ln_mix_g[l], w_in[l])
        y_lru = _rglru(xb, gb, conv_w[l], conv_b[l], w_gate_a[l], b_gate_a[l], w_gate_x[l], b_gate_x[l],
                       lru_L[l], lru_out_g[l], batch, seq)
        y_att = _swa(q, k, v, bias, q_norm_g[l], k_norm_g[l], sinks[l], attn_out_g[l], batch, seq)
        x1, xn = _outproj(x2, y_lru, y_att, w_out[l], ln_ffn_g[l])
        off, gate = _peer_topk(xn, w_query[l], sub_keys[l])
        tab_u, tab_v = _pack_tables(expert_u[l], expert_v[l])
        w = _peer_u(off, gate, xn, tab_u)
        x2 = _peer_v(off, w, x1, tab_v)
    return x2.reshape(batch, seq, D_MODEL)
```
